```python
import jax, jax.numpy as jnp
from jax import lax
import numpy as np

D_MODEL = 1024
BATCH = 8
SEQ = 2048
DEPTH = 1

ATTN_HEADS = 8
HEAD_DIM = 64
D_ATTN = ATTN_HEADS * HEAD_DIM
MOBA_BLOCK = 256
MOBA_TOPK = 3
Q_CHUNK = 32
ROPE_THETA = 10000.0
POOL_WINDOWS = (2, 4, 8, 16)
N_POOL_GROUPS = len(POOL_WINDOWS)
POOL_GROUP_DIM = 128
D_POOL = N_POOL_GROUPS * POOL_GROUP_DIM
N_BRANCHES = 2
D_IN_PROJ = 3 * D_ATTN + D_POOL + N_BRANCHES * D_MODEL
D_FF = 2816
CONV_WIDTH = 3
LN_EPS = 1e-5
DEEPNORM_ALPHA = (2.0 * DEPTH) ** 0.25
DEEPNORM_BETA = (8.0 * DEPTH) ** -0.25
NEG = -1e30

kernel_name = "hybrid_moba_pool_convffn_deepnorm"


def layer_norm(x, g, b):
    xf = x.astype(jnp.float32)
    mu = jnp.mean(xf, axis=-1, keepdims=True)
    var = jnp.mean(jnp.square(xf - mu), axis=-1, keepdims=True)
    y = (xf - mu) * lax.rsqrt(var + LN_EPS) * g.astype(jnp.float32) + b.astype(jnp.float32)
    return y.astype(x.dtype)


def rope_tables(s):
    half = HEAD_DIM // 2
    inv_freq = 1.0 / (ROPE_THETA ** (jnp.arange(half, dtype=jnp.float32) / half))
    ang = jnp.arange(s, dtype=jnp.float32)[:, None] * inv_freq[None, :]
    return jnp.cos(ang), jnp.sin(ang)


def apply_rope(t, cos, sin):
    tf = t.astype(jnp.float32)
    half = HEAD_DIM // 2
    t1, t2 = tf[..., :half], tf[..., half:]
    return jnp.concatenate([t1 * cos - t2 * sin, t2 * cos + t1 * sin], axis=-1).astype(t.dtype)


def moba_attention(q, k, v):
    b, h, s, d = q.shape
    nb = -(-s // MOBA_BLOCK)
    s_pad = nb * MOBA_BLOCK
    pad = ((0, 0), (0, 0), (0, s_pad - s), (0, 0))
    q, k, v = jnp.pad(q, pad), jnp.pad(k, pad), jnp.pad(v, pad)
    k_blocks = k.reshape(b, h, nb, MOBA_BLOCK, d)
    v_blocks = v.reshape(b, h, nb, MOBA_BLOCK, d)
    k_mean = jnp.mean(k_blocks.astype(jnp.float32), axis=3)
    q_block = jnp.arange(s_pad) // MOBA_BLOCK
    gate = jnp.einsum('bhsd,bhnd->bhsn', q.astype(jnp.float32), k_mean)
    past = jnp.arange(nb)[None, :] < q_block[:, None]
    gate = jnp.where(past, gate, NEG)
    n_sel = min(MOBA_TOPK, nb)
    _, sel = lax.top_k(gate, n_sel)
    sel_valid = sel < q_block[:, None]

    nc = s_pad // Q_CHUNK

    def to_chunks(t):
        return jnp.moveaxis(t.reshape(b, h, nc, Q_CHUNK, *t.shape[3:]), 2, 0)

    gather = jax.vmap(jax.vmap(lambda blocks, idx: blocks[idx]))
    scale = HEAD_DIM ** -0.5
    key_off = jnp.arange(MOBA_BLOCK)

    def one_chunk(args):
        c, qc, sc, vc = args
        qpos = c * Q_CHUNK + jnp.arange(Q_CHUNK)
        own = (c * Q_CHUNK) // MOBA_BLOCK
        k_own = lax.dynamic_index_in_dim(k_blocks, own, axis=2, keepdims=False).astype(jnp.float32)
        v_own = lax.dynamic_index_in_dim(v_blocks, own, axis=2, keepdims=False).astype(jnp.float32)
        k_sel = gather(k_blocks, sc).astype(jnp.float32)
        v_sel = gather(v_blocks, sc).astype(jnp.float32)
        qf = qc.astype(jnp.float32) * scale
        s_sel = jnp.einsum('bhqd,bhqnkd->bhqnk', qf, k_sel)
        s_sel = jnp.where(vc[..., None], s_sel, NEG).reshape(b, h, Q_CHUNK, n_sel * MOBA_BLOCK)
        s_own = jnp.einsum('bhqd,bhkd->bhqk', qf, k_own)
        causal = (own * MOBA_BLOCK + key_off)[None, :] <= qpos[:, None]
        s_own = jnp.where(causal, s_own, NEG)
        p = jax.nn.softmax(jnp.concatenate([s_sel, s_own], axis=-1), axis=-1)
        p_sel = p[..., :n_sel * MOBA_BLOCK].reshape(b, h, Q_CHUNK, n_sel, MOBA_BLOCK)
        p_own = p[..., n_sel * MOBA_BLOCK:]
        o = (jnp.einsum('bhqnk,bhqnkd->bhqd', p_sel, v_sel)
             + jnp.einsum('bhqk,bhkd->bhqd', p_own, v_own))
        return o.astype(q.dtype)

    out = lax.map(one_chunk, (jnp.arange(nc), to_chunks(q), to_chunks(sel), to_chunks(sel_valid)))
    out = jnp.moveaxis(out, 0, 2).reshape(b, h, s_pad, d)
    return out[:, :, :s]


def multiscale_pool(u, w_pool, pool_scale):
    b, s, _ = u.shape
    uf = u.astype(jnp.float32)
    cs = jnp.pad(lax.cumsum(uf, axis=1), ((0, 0), (1, 0), (0, 0)))
    t = jnp.arange(s)
    outs = []
    for g, w in enumerate(POOL_WINDOWS):
        sl = slice(g * POOL_GROUP_DIM, (g + 1) * POOL_GROUP_DIM)
        start = jnp.maximum(t + 1 - w, 0)
        count = (t + 1 - start).astype(jnp.float32)
        win_sum = cs[:, 1:, sl] - cs[:, start, sl]
        outs.append(win_sum / count[None, :, None] - uf[:, :, sl])
    pooled = jnp.stack(outs, axis=2)
    mixed = jnp.einsum('bsgc,gcd->bsgd', pooled, w_pool.astype(jnp.float32)).reshape(b, s, D_POOL)
    return (mixed * pool_scale.astype(jnp.float32)).astype(u.dtype)


def conv_ffn(x, w_ffn_gate, w_ffn_up, conv_w, conv_b, w_ffn_down):
    s = x.shape[1]
    a = x @ w_ffn_gate
    u = x @ w_ffn_up
    ap = jnp.pad(a, ((0, 0), (CONV_WIDTH - 1, 0), (0, 0)))
    a = sum(ap[:, i:i + s] * conv_w[i] for i in range(CONV_WIDTH)) + conv_b
    return (jax.nn.gelu(a, approximate=False) * u) @ w_ffn_down


def setup_inputs(seed: int = 0) -> dict:
    key = jax.random.key(seed)
    ks = jax.random.split(key, 20)
    L, D = DEPTH, D_MODEL
    f32 = jnp.float32

    def nrm(k, shape, fan_in, gain=1.0):
        return (jax.random.normal(k, shape, f32) * (gain * fan_in ** -0.5)).astype(f32)

    return {
        "x": jax.random.normal(ks[0], (BATCH, SEQ, D), f32),
        "w_in": nrm(ks[1], (L, D, D_IN_PROJ), D),
        "b_gate": 0.02 * jax.random.normal(ks[2], (L, N_BRANCHES * D), f32),
        "w_branch_attn": nrm(ks[3], (L, D_ATTN, D), D_ATTN),
        "w_pool": nrm(ks[4], (L, N_POOL_GROUPS, POOL_GROUP_DIM, POOL_GROUP_DIM), POOL_GROUP_DIM),
        "pool_scale": 1.0 + 0.05 * jax.random.normal(ks[5], (L, D_POOL), f32),
        "w_branch_pool": nrm(ks[6], (L, D_POOL, D), D_POOL),
        "w_out": nrm(ks[7], (L, D, D), D, DEEPNORM_BETA),
        "ln1_g": 1.0 + 0.05 * jax.random.normal(ks[8], (L, D), f32),
        "ln1_b": 0.02 * jax.random.normal(ks[9], (L, D), f32),
        "w_ffn_gate": nrm(ks[10], (L, D, D_FF), D),
        "w_ffn_up": nrm(ks[11], (L, D, D_FF), D),
        "conv_w": nrm(ks[12], (L, CONV_WIDTH, D_FF), CONV_WIDTH),
        "conv_b": 0.02 * jax.random.normal(ks[13], (L, D_FF), f32),
        "w_ffn_down": nrm(ks[14], (L, D_FF, D), D_FF, DEEPNORM_BETA),
        "ln2_g": 1.0 + 0.05 * jax.random.normal(ks[15], (L, D), f32),
        "ln2_b": 0.02 * jax.random.normal(ks[16], (L, D), f32),
    }


def reference(x, w_in, b_gate, w_branch_attn, w_pool, pool_scale, w_branch_pool, w_out,
              ln1_g, ln1_b, w_ffn_gate, w_ffn_up, conv_w, conv_b, w_ffn_down, ln2_g, ln2_b):
    b, s, _ = x.shape
    cos, sin = rope_tables(s)
    for l in range(DEPTH):
        z = x @ w_in[l]
        o0, o1, o2, o3 = D_ATTN, 2 * D_ATTN, 3 * D_ATTN, 3 * D_ATTN + D_POOL
        heads = lambda t: t.reshape(b, s, ATTN_HEADS, HEAD_DIM).transpose(0, 2, 1, 3)
        q = apply_rope(heads(z[..., :o0]), cos, sin)
        k = apply_rope(heads(z[..., o0:o1]), cos, sin)
        v = heads(z[..., o1:o2])
        u_pool = z[..., o2:o3]
        gates = jax.nn.sigmoid(z[..., o3:] + b_gate[l])
        g_attn, g_pool = gates[..., :D_MODEL], gates[..., D_MODEL:]
        y_attn = moba_attention(q, k, v).transpose(0, 2, 1, 3).reshape(b, s, D_ATTN)
        y_attn = y_attn @ w_branch_attn[l]
        y_pool = multiscale_pool(u_pool, w_pool[l], pool_scale[l]) @ w_branch_pool[l]
        mix = (g_attn * y_attn + g_pool * y_pool) @ w_out[l]
        x = layer_norm(DEEPNORM_ALPHA * x + mix, ln1_g[l], ln1_b[l])
        ffn = conv_ffn(x, w_ffn_gate[l], w_ffn_up[l], conv_w[l], conv_b[l], w_ffn_down[l])
        x = layer_norm(DEEPNORM_ALPHA * x + ffn, ln2_g[l], ln2_b[l])
    return x
```

```python
import functools

import jax
import jax.numpy as jnp
from jax import lax
from jax.experimental import pallas as pl
from jax.experimental.pallas import tpu as pltpu

D_MODEL = 1024
ATTN_HEADS = 8
HEAD_DIM = 64
D_ATTN = ATTN_HEADS * HEAD_DIM
MOBA_BLOCK = 256
MOBA_TOPK = 3
ROPE_THETA = 10000.0
POOL_WINDOWS = (2, 4, 8, 16)
POOL_GROUP_DIM = 128
D_POOL = len(POOL_WINDOWS) * POOL_GROUP_DIM
D_FF = 2816
CONV_WIDTH = 3
LN_EPS = 1e-5
NEG = -1e30

LANES = 128
SUBLANES = 8
HEADS_PER_VREG = LANES // HEAD_DIM
POOL_HALO = 16
ROW_TILE = 512
FF_CHUNK = 256
VMEM_LIMIT = 52 * 1024 * 1024

F32 = jnp.float32
BF16 = jnp.bfloat16


def _dot(a, b):
    return jnp.dot(a, b, preferred_element_type=F32)


def _dot_nt(a, b):
    return lax.dot_general(a, b, (((1,), (1,)), ((), ())), preferred_element_type=F32)


def _const_spec(shape):
    return pl.BlockSpec(shape, lambda *_: (0,) * len(shape), pipeline_mode=pl.Buffered(1))


def _layer_norm(y, g, b):
    mu = jnp.mean(y, axis=-1, keepdims=True)
    d = y - mu
    var = jnp.mean(d * d, axis=-1, keepdims=True)
    return d * lax.rsqrt(var + LN_EPS) * g + b


def _inproj_kernel(x_ref, w_ref, bg_ref, cos_ref, sin_ref, wpool_ref, pscale_ref,
                   q_ref, k_ref, v_ref, pm_ref, g_ref, ubuf):
    i = pl.program_id(1)
    tm = x_ref.shape[1]
    xb = x_ref[0].astype(BF16)

    lane = lax.broadcasted_iota(jnp.int32, (tm, LANES), 1)
    first_half = (lane % HEAD_DIM) < (HEAD_DIM // 2)
    cos = cos_ref[...]
    sin = sin_ref[...]

    def rope(t):
        partner = jnp.where(first_half, pltpu.roll(t, LANES - HEAD_DIM // 2, 1),
                            pltpu.roll(t, HEAD_DIM // 2, 1))
        return t * cos + partner * sin

    zq = _dot(xb, w_ref[:, 0:D_ATTN])
    for c in range(D_ATTN // LANES):
        sl = slice(c * LANES, (c + 1) * LANES)
        q_ref[0, :, sl] = (rope(zq[:, sl]) * (HEAD_DIM ** -0.5)).astype(BF16)
    zk = _dot(xb, w_ref[:, D_ATTN:2 * D_ATTN])
    for c in range(D_ATTN // LANES):
        sl = slice(c * LANES, (c + 1) * LANES)
        k_ref[0, :, sl] = rope(zk[:, sl]).astype(BF16)
    v_ref[0] = _dot(xb, w_ref[:, 2 * D_ATTN:3 * D_ATTN]).astype(BF16)

    @pl.when(i == 0)
    def _():
        ubuf[0:POOL_HALO, :] = jnp.zeros((POOL_HALO, D_POOL), F32)

    o3 = 3 * D_ATTN
    ubuf[POOL_HALO:POOL_HALO + tm, :] = _dot(xb, w_ref[:, o3:o3 + D_POOL])
    pos = i * tm + lax.broadcasted_iota(jnp.int32, (tm, 1), 0)
    for g, w in enumerate(POOL_WINDOWS):
        sl = slice(g * POOL_GROUP_DIM, (g + 1) * POOL_GROUP_DIM)
        tok = ubuf[POOL_HALO:POOL_HALO + tm, sl]
        win = tok
        for d in range(1, w):
            win = win + ubuf[POOL_HALO - d:POOL_HALO - d + tm, sl]
        count = jnp.minimum(pos + 1, w).astype(F32)
        pooled = win / count - tok
        mixed = _dot(pooled.astype(BF16), wpool_ref[g]) * pscale_ref[:, sl]
        pm_ref[0, :, sl] = mixed.astype(BF16)
    ubuf[0:POOL_HALO, :] = ubuf[tm:tm + POOL_HALO, :]

    o4 = o3 + D_POOL
    for c in range(2 * D_MODEL // D_ATTN):
        sl = slice(c * D_ATTN, (c + 1) * D_ATTN)
        zg = _dot(xb, w_ref[:, o4 + c * D_ATTN:o4 + (c + 1) * D_ATTN]) + bg_ref[:, sl]
        g_ref[0, :, sl] = jax.nn.sigmoid(zg).astype(BF16)


def _inproj(x, w_in, b_gate, cos, sin, w_pool, pool_scale):
    b, s, d = x.shape
    tm = ROW_TILE
    n_in = w_in.shape[1]
    row = lambda n: pl.BlockSpec((1, tm, n), lambda bi, i: (bi, i, 0))
    return pl.pallas_call(
        _inproj_kernel,
        grid=(b, s // tm),
        in_specs=[
            row(d),
            _const_spec((d, n_in)),
            _const_spec((1, 2 * D_MODEL)),
            pl.BlockSpec((tm, LANES), lambda bi, i: (i, 0)),
            pl.BlockSpec((tm, LANES), lambda bi, i: (i, 0)),
            _const_spec(w_pool.shape),
            _const_spec((1, D_POOL)),
        ],
        out_specs=[row(D_ATTN), row(D_ATTN), row(D_ATTN), row(D_POOL), row(2 * D_MODEL)],
        out_shape=[
            jax.ShapeDtypeStruct((b, s, D_ATTN), BF16),
            jax.ShapeDtypeStruct((b, s, D_ATTN), BF16),
            jax.ShapeDtypeStruct((b, s, D_ATTN), BF16),
            jax.ShapeDtypeStruct((b, s, D_POOL), BF16),
            jax.ShapeDtypeStruct((b, s, 2 * D_MODEL), BF16),
        ],
        scratch_shapes=[pltpu.VMEM((POOL_HALO + tm, D_POOL), F32)],
        compiler_params=pltpu.CompilerParams(
            dimension_semantics=("arbitrary", "arbitrary"), vmem_limit_bytes=VMEM_LIMIT),
        name="inproj",
    )(x, w_in, b_gate, cos, sin, w_pool, pool_scale)


def _moba_kernel(q_ref, k_ref, v_ref, o_ref, km_ref):
    i = pl.program_id(2)
    tq = q_ref.shape[1]
    nb = k_ref.shape[1] // MOBA_BLOCK

    @pl.when(i == 0)
    def _():
        kf = k_ref[0].astype(F32).reshape(nb, MOBA_BLOCK, LANES)
        km = jnp.sum(kf, axis=1) * (1.0 / MOBA_BLOCK)
        hi = km.astype(BF16).astype(F32)
        km_ref[...] = jnp.zeros(km_ref.shape, F32)
        km_ref[0:nb, :] = hi
        km_ref[nb:2 * nb, :] = km - hi

    q = q_ref[0]
    kmb = km_ref[...].astype(BF16)
    lane = lax.broadcasted_iota(jnp.int32, (tq, LANES), 1)
    row = lax.broadcasted_iota(jnp.int32, (tq, MOBA_BLOCK), 0)
    col = lax.broadcasted_iota(jnp.int32, (tq, MOBA_BLOCK), 1)
    causal = col <= row
    own = pl.ds(pl.multiple_of(i * MOBA_BLOCK, MOBA_BLOCK), MOBA_BLOCK)

    outs = []
    for h in range(HEADS_PER_VREG):
        in_head = (lane // HEAD_DIM) == h
        qh = jnp.where(in_head, q, jnp.zeros_like(q))

        g2 = _dot_nt(qh, kmb)
        gate = g2 + pltpu.roll(g2, LANES - nb, 1)
        beaten = jnp.zeros((tq, LANES), jnp.int32)
        for m in range(nb):
            gm = jnp.sum(jnp.where(lane == m, gate, 0.0), axis=1, keepdims=True)
            wins = (gm > gate) | ((gm == gate) & (m < lane))
            beaten = beaten + jnp.where(wins & (m < i), 1, 0)
        sel = jnp.where((lane < i) & (beaten < MOBA_TOPK), 1.0, 0.0)

        s = jnp.where(causal, _dot_nt(qh, k_ref[0, own, :]), NEG)
        m0 = jnp.max(s, axis=1, keepdims=True)
        p = jnp.exp(s - m0)
        l0 = jnp.sum(p, axis=1, keepdims=True)
        acc0 = _dot(p.astype(BF16), v_ref[0, own, :])

        def past_block(j, carry):
            m_prev, l_prev, acc = carry
            blk = pl.ds(pl.multiple_of(j * MOBA_BLOCK, MOBA_BLOCK), MOBA_BLOCK)
            picked = jnp.sum(jnp.where(lane == j, sel, 0.0), axis=1, keepdims=True) > 0.5
            s = jnp.where(picked, _dot_nt(qh, k_ref[0, blk, :]), NEG)
            m_new = jnp.maximum(m_prev, jnp.max(s, axis=1, keepdims=True))
            alpha = jnp.exp(m_prev - m_new)
            p = jnp.exp(s - m_new)
            l_new = alpha * l_prev + jnp.sum(p, axis=1, keepdims=True)
            acc = alpha * acc + _dot(p.astype(BF16), v_ref[0, blk, :])
            return m_new, l_new, acc

        _, l_fin, acc = lax.fori_loop(0, i, past_block, (m0, l0, acc0))
        outs.append(acc / l_fin)

    o_ref[0] = jnp.where((lane // HEAD_DIM) == 0, outs[0], outs[1]).astype(BF16)


def _moba(q, k, v):
    b, s, _ = q.shape
    n_pairs = D_ATTN // LANES
    blk = pl.BlockSpec((1, MOBA_BLOCK, LANES), lambda bi, p, i: (bi, i, p))
    full = pl.BlockSpec((1, s, LANES), lambda bi, p, i: (bi, 0, p))
    return pl.pallas_call(
        _moba_kernel,
        grid=(b, n_pairs, s // MOBA_BLOCK),
        in_specs=[blk, full, full],
        out_specs=blk,
        out_shape=jax.ShapeDtypeStruct((b, s, D_ATTN), BF16),
        scratch_shapes=[pltpu.VMEM((LANES, LANES), F32)],
        compiler_params=pltpu.CompilerParams(
            dimension_semantics=("arbitrary", "arbitrary", "arbitrary"),
            vmem_limit_bytes=VMEM_LIMIT),
        name="moba",
    )(q, k, v)


def _merge_kernel(alpha, x_ref, ya_ref, pm_ref, g_ref, wa_ref, wp_ref, wo_ref, lg_ref, lb_ref, o_ref):
    y_attn = _dot(ya_ref[0], wa_ref[...])
    y_pool = _dot(pm_ref[0], wp_ref[...])
    g_attn = g_ref[0, :, 0:D_MODEL].astype(F32)
    g_pool = g_ref[0, :, D_MODEL:2 * D_MODEL].astype(F32)
    mix = _dot((g_attn * y_attn + g_pool * y_pool).astype(BF16), wo_ref[...])
    o_ref[0] = _layer_norm(alpha * x_ref[0] + mix, lg_ref[...], lb_ref[...])


def _merge(alpha, x, y_attn, pm, gates, w_a, w_p, w_o, ln_g, ln_b):
    b, s, d = x.shape
    tm = ROW_TILE
    row = lambda n: pl.BlockSpec((1, tm, n), lambda bi, i: (bi, i, 0))
    return pl.pallas_call(
        functools.partial(_merge_kernel, alpha),
        grid=(b, s // tm),
        in_specs=[row(d), row(D_ATTN), row(D_POOL), row(2 * D_MODEL),
                  _const_spec(w_a.shape), _const_spec(w_p.shape), _const_spec(w_o.shape),
                  _const_spec((1, d)), _const_spec((1, d))],
        out_specs=row(d),
        out_shape=jax.ShapeDtypeStruct((b, s, d), F32),
        compiler_params=pltpu.CompilerParams(
            dimension_semantics=("arbitrary", "arbitrary"), vmem_limit_bytes=VMEM_LIMIT),
        name="merge",
    )(x, y_attn, pm, gates, w_a, w_p, w_o, ln_g, ln_b)


def _ffn_kernel(alpha, x_ref, wg_ref, wu_ref, cw_ref, cb_ref, wd_ref, lg_ref, lb_ref, o_ref,
                abuf, carry):
    i = pl.program_id(1)
    tm = x_ref.shape[1]
    halo = SUBLANES

    @pl.when(i == 0)
    def _():
        carry[...] = jnp.zeros(carry.shape, F32)

    x = x_ref[0]
    xb = x.astype(BF16)
    acc = jnp.zeros((tm, D_MODEL), F32)
    for c in range(D_FF // FF_CHUNK):
        sl = slice(c * FF_CHUNK, (c + 1) * FF_CHUNK)
        a = _dot(xb, wg_ref[:, sl])
        u = _dot(xb, wu_ref[:, sl])
        abuf[0:halo, :] = carry[:, sl]
        abuf[halo:halo + tm, :] = a
        carry[:, sl] = abuf[tm:tm + halo, :]
        conv = a * cw_ref[CONV_WIDTH - 1:CONV_WIDTH, sl] + cb_ref[:, sl]
        for t in range(CONV_WIDTH - 1):
            back = CONV_WIDTH - 1 - t
            conv = conv + abuf[halo - back:halo - back + tm, :] * cw_ref[t:t + 1, sl]
        gelu = 0.5 * conv * (1.0 + lax.erf(conv * (2.0 ** -0.5)))
        acc = acc + _dot((gelu * u).astype(BF16), wd_ref[sl, :])
    o_ref[0] = _layer_norm(alpha * x + acc, lg_ref[...], lb_ref[...])


def _ffn(alpha, x, w_g, w_u, conv_w, conv_b, w_d, ln_g, ln_b):
    b, s, d = x.shape
    tm = ROW_TILE
    row = pl.BlockSpec((1, tm, d), lambda bi, i: (bi, i, 0))
    return pl.pallas_call(
        functools.partial(_ffn_kernel, alpha),
        grid=(b, s // tm),
        in_specs=[row, _const_spec(w_g.shape), _const_spec(w_u.shape),
                  _const_spec(conv_w.shape), _const_spec((1, D_FF)), _const_spec(w_d.shape),
                  _const_spec((1, d)), _const_spec((1, d))],
        out_specs=row,
        out_shape=jax.ShapeDtypeStruct((b, s, d), F32),
        scratch_shapes=[pltpu.VMEM((SUBLANES + tm, FF_CHUNK), F32),
                        pltpu.VMEM((SUBLANES, D_FF), F32)],
        compiler_params=pltpu.CompilerParams(
            dimension_semantics=("arbitrary", "arbitrary"), vmem_limit_bytes=VMEM_LIMIT),
        name="convffn",
    )(x, w_g, w_u, conv_w, conv_b, w_d, ln_g, ln_b)


def _rope_tables(s):
    half = HEAD_DIM // 2
    inv_freq = 1.0 / (ROPE_THETA ** (jnp.arange(half, dtype=F32) / half))
    ang = jnp.arange(s, dtype=F32)[:, None] * inv_freq[None, :]
    cos, sin = jnp.cos(ang), jnp.sin(ang)
    cos = jnp.tile(jnp.concatenate([cos, cos], axis=-1), (1, HEADS_PER_VREG))
    sin = jnp.tile(jnp.concatenate([-sin, sin], axis=-1), (1, HEADS_PER_VREG))
    return cos, sin


def kernel(x, w_in, b_gate, w_branch_attn, w_pool, pool_scale, w_branch_pool, w_out, ln1_g, ln1_b,
           w_ffn_gate, w_ffn_up, conv_w, conv_b, w_ffn_down, ln2_g, ln2_b):
    depth = w_in.shape[0]
    s = x.shape[1]
    alpha = (2.0 * depth) ** 0.25
    cos, sin = _rope_tables(s)
    vec = lambda t: t.reshape(1, -1)
    for l in range(depth):
        q, k, v, pm, gates = _inproj(x, w_in[l].astype(BF16), vec(b_gate[l]), cos, sin,
                                     w_pool[l].astype(BF16), vec(pool_scale[l]))
        y_attn = _moba(q, k, v)
        x = _merge(alpha, x, y_attn, pm, gates, w_branch_attn[l].astype(BF16),
                   w_branch_pool[l].astype(BF16), w_out[l].astype(BF16), vec(ln1_g[l]), vec(ln1_b[l]))
        x = _ffn(alpha, x, w_ffn_gate[l].astype(BF16), w_ffn_up[l].astype(BF16), conv_w[l],
                 vec(conv_b[l]), w_ffn_down[l].astype(BF16), vec(ln2_g[l]), vec(ln2_b[l]))
    return x
```

```python
import functools

import jax
import jax.numpy as jnp
from jax import lax
from jax.experimental import pallas as pl
from jax.experimental.pallas import tpu as pltpu

D_MODEL = 1024
ATTN_HEADS = 8
HEAD_DIM = 64
D_ATTN = ATTN_HEADS * HEAD_DIM
MOBA_BLOCK = 256
MOBA_TOPK = 3
ROPE_THETA = 10000.0
POOL_WINDOWS = (2, 4, 8, 16)
POOL_GROUP_DIM = 128
D_POOL = len(POOL_WINDOWS) * POOL_GROUP_DIM
D_FF = 2816
CONV_WIDTH = 3
LN_EPS = 1e-5
NEG = -1e30

LANES = 128
SUBLANES = 8
HEADS_PER_VREG = LANES // HEAD_DIM
POOL_HALO = 16
ROW_TILE = 512
FF_CHUNK = 256
VMEM_LIMIT = 52 * 1024 * 1024

F32 = jnp.float32
BF16 = jnp.bfloat16


def _dot(a, b):
    return jnp.dot(a, b, preferred_element_type=F32)


def _dot_nt(a, b):
    return lax.dot_general(a, b, (((1,), (1,)), ((), ())), preferred_element_type=F32)


def _const_spec(shape):
    return pl.BlockSpec(shape, lambda *_: (0,) * len(shape), pipeline_mode=pl.Buffered(1))


def _layer_norm(y, g, b):
    mu = jnp.mean(y, axis=-1, keepdims=True)
    d = y - mu
    var = jnp.mean(d * d, axis=-1, keepdims=True)
    return d * lax.rsqrt(var + LN_EPS) * g + b


def _inproj_kernel(x_ref, w_ref, bg_ref, cos_ref, sin_ref, wpool_ref, pscale_ref,
                   q_ref, k_ref, v_ref, pm_ref, g_ref, ubuf):
    i = pl.program_id(1)
    tm = x_ref.shape[1]
    xb = x_ref[0].astype(BF16)

    lane = lax.broadcasted_iota(jnp.int32, (tm, LANES), 1)
    first_half = (lane % HEAD_DIM) < (HEAD_DIM // 2)
    cos = cos_ref[...]
    sin = sin_ref[...]

    def rope(t):
        partner = jnp.where(first_half, pltpu.roll(t, LANES - HEAD_DIM // 2, 1),
                            pltpu.roll(t, HEAD_DIM // 2, 1))
        return t * cos + partner * sin

    zq = _dot(xb, w_ref[:, 0:D_ATTN])
    for c in range(D_ATTN // LANES):
        sl = slice(c * LANES, (c + 1) * LANES)
        q_ref[0, :, sl] = (rope(zq[:, sl]) * (HEAD_DIM ** -0.5)).astype(BF16)
    zk = _dot(xb, w_ref[:, D_ATTN:2 * D_ATTN])
    for c in range(D_ATTN // LANES):
        sl = slice(c * LANES, (c + 1) * LANES)
        k_ref[0, :, sl] = rope(zk[:, sl]).astype(BF16)
    v_ref[0] = _dot(xb, w_ref[:, 2 * D_ATTN:3 * D_ATTN]).astype(BF16)

    @pl.when(i == 0)
    def _():
        ubuf[0:POOL_HALO, :] = jnp.zeros((POOL_HALO, D_POOL), F32)

    o3 = 3 * D_ATTN
    ubuf[POOL_HALO:POOL_HALO + tm, :] = _dot(xb, w_ref[:, o3:o3 + D_POOL])
    pos = i * tm + lax.broadcasted_iota(jnp.int32, (tm, 1), 0)
    for g, w in enumerate(POOL_WINDOWS):
        sl = slice(g * POOL_GROUP_DIM, (g + 1) * POOL_GROUP_DIM)
        tok = ubuf[POOL_HALO:POOL_HALO + tm, sl]
        win = tok
        for d in range(1, w):
            win = win + ubuf[POOL_HALO - d:POOL_HALO - d + tm, sl]
        count = jnp.minimum(pos + 1, w).astype(F32)
        pooled = win / count - tok
        mixed = _dot(pooled.astype(BF16), wpool_ref[g]) * pscale_ref[:, sl]
        pm_ref[0, :, sl] = mixed.astype(BF16)
    ubuf[0:POOL_HALO, :] = ubuf[tm:tm + POOL_HALO, :]

    o4 = o3 + D_POOL
    for c in range(2 * D_MODEL // D_ATTN):
        sl = slice(c * D_ATTN, (c + 1) * D_ATTN)
        zg = _dot(xb, w_ref[:, o4 + c * D_ATTN:o4 + (c + 1) * D_ATTN]) + bg_ref[:, sl]
        g_ref[0, :, sl] = jax.nn.sigmoid(zg).astype(BF16)


def _inproj(x, w_in, b_gate, cos, sin, w_pool, pool_scale):
    b, s, d = x.shape
    tm = ROW_TILE
    n_in = w_in.shape[1]
    row = lambda n: pl.BlockSpec((1, tm, n), lambda bi, i: (bi, i, 0))
    return pl.pallas_call(
        _inproj_kernel,
        grid=(b, s // tm),
        in_specs=[
            row(d),
            _const_spec((d, n_in)),
            _const_spec((1, 2 * D_MODEL)),
            pl.BlockSpec((tm, LANES), lambda bi, i: (i, 0)),
            pl.BlockSpec((tm, LANES), lambda bi, i: (i, 0)),
            _const_spec(w_pool.shape),
            _const_spec((1, D_POOL)),
        ],
        out_specs=[row(D_ATTN), row(D_ATTN), row(D_ATTN), row(D_POOL), row(2 * D_MODEL)],
        out_shape=[
            jax.ShapeDtypeStruct((b, s, D_ATTN), BF16),
            jax.ShapeDtypeStruct((b, s, D_ATTN), BF16),
            jax.ShapeDtypeStruct((b, s, D_ATTN), BF16),
            jax.ShapeDtypeStruct((b, s, D_POOL), BF16),
            jax.ShapeDtypeStruct((b, s, 2 * D_MODEL), BF16),
        ],
        scratch_shapes=[pltpu.VMEM((POOL_HALO + tm, D_POOL), F32)],
        compiler_params=pltpu.CompilerParams(
            dimension_semantics=("arbitrary", "arbitrary"), vmem_limit_bytes=VMEM_LIMIT),
        name="inproj",
    )(x, w_in, b_gate, cos, sin, w_pool, pool_scale)


def _moba_kernel(q_ref, k_ref, v_ref, o_ref, vt_ref, s_ref):
    seq = k_ref.shape[1]
    nb = seq // MOBA_BLOCK
    blk = lambda j: slice(j * MOBA_BLOCK, (j + 1) * MOBA_BLOCK)

    kf = k_ref[0].astype(F32).reshape(nb, MOBA_BLOCK, LANES)
    km = jnp.sum(kf, axis=1) * (1.0 / MOBA_BLOCK)
    hi = km.astype(BF16)
    kmb = jnp.concatenate([hi, (km - hi.astype(F32)).astype(BF16)], axis=0)

    for j in range(nb):
        vt_ref[:, blk(j)] = v_ref[0, blk(j), :].astype(F32).T.astype(BF16)

    lane = lax.broadcasted_iota(jnp.int32, (MOBA_BLOCK, LANES), 1)
    key_i = lax.broadcasted_iota(jnp.int32, (MOBA_BLOCK, MOBA_BLOCK), 0)
    qry_i = lax.broadcasted_iota(jnp.int32, (MOBA_BLOCK, MOBA_BLOCK), 1)
    causal = key_i <= qry_i
    blk_i = lax.broadcasted_iota(jnp.int32, (nb, MOBA_BLOCK), 0)
    fold = lambda t: t.reshape(MOBA_BLOCK // SUBLANES, SUBLANES, MOBA_BLOCK)

    for i in range(nb):
        q = q_ref[0, blk(i), :]
        outs = []
        for h in range(HEADS_PER_VREG):
            qh = jnp.where((lane // HEAD_DIM) == h, q, jnp.zeros_like(q))
            if i > MOBA_TOPK:
                g2 = _dot_nt(kmb, qh)
                gate = g2[0:nb] + g2[nb:2 * nb]
                beaten = jnp.zeros((nb, MOBA_BLOCK), jnp.int32)
                for m in range(i):
                    gm = gate[m:m + 1, :]
                    wins = (gm > gate) | ((gm == gate) & (m < blk_i))
                    beaten = beaten + jnp.where(wins, 1, 0)
                sel = jnp.where(beaten < MOBA_TOPK, 1.0, 0.0)
            m_run = jnp.full((SUBLANES, MOBA_BLOCK), NEG, F32)
            for j in range(i + 1):
                s = _dot_nt(k_ref[0, blk(j), :], qh)
                if j == i:
                    s = jnp.where(causal, s, NEG)
                elif i > MOBA_TOPK:
                    s = jnp.where(sel[j:j + 1, :] > 0.5, s, NEG)
                s_ref[h, blk(j), :] = s
                m_run = jnp.maximum(m_run, jnp.max(fold(s), axis=0))
            m_fin = jnp.max(m_run, axis=0, keepdims=True)
            l_run = jnp.zeros((SUBLANES, MOBA_BLOCK), F32)
            acc = jnp.zeros((HEAD_DIM, MOBA_BLOCK), F32)
            for j in range(i + 1):
                p = jnp.exp(s_ref[h, blk(j), :] - m_fin)
                l_run = l_run + jnp.sum(fold(p), axis=0)
                acc = acc + _dot(vt_ref[h * HEAD_DIM:(h + 1) * HEAD_DIM, blk(j)], p.astype(BF16))
            outs.append(acc / jnp.sum(l_run, axis=0, keepdims=True))
        o_ref[0, blk(i), :] = jnp.concatenate(outs, axis=0).T.astype(BF16)


def _moba(q, k, v):
    b, s, _ = q.shape
    full = pl.BlockSpec((1, s, LANES), lambda bi, p: (bi, 0, p))
    return pl.pallas_call(
        _moba_kernel,
        grid=(b, D_ATTN // LANES),
        in_specs=[full, full, full],
        out_specs=full,
        out_shape=jax.ShapeDtypeStruct((b, s, D_ATTN), BF16),
        scratch_shapes=[pltpu.VMEM((LANES, s), BF16),
                        pltpu.VMEM((HEADS_PER_VREG, s, MOBA_BLOCK), F32)],
        compiler_params=pltpu.CompilerParams(
            dimension_semantics=("arbitrary", "arbitrary"), vmem_limit_bytes=VMEM_LIMIT),
        name="moba",
    )(q, k, v)


def _merge_kernel(alpha, x_ref, ya_ref, pm_ref, g_ref, wa_ref, wp_ref, wo_ref, lg_ref, lb_ref, o_ref):
    y_attn = _dot(ya_ref[0], wa_ref[...])
    y_pool = _dot(pm_ref[0], wp_ref[...])
    g_attn = g_ref[0, :, 0:D_MODEL].astype(F32)
    g_pool = g_ref[0, :, D_MODEL:2 * D_MODEL].astype(F32)
    mix = _dot((g_attn * y_attn + g_pool * y_pool).astype(BF16), wo_ref[...])
    o_ref[0] = _layer_norm(alpha * x_ref[0] + mix, lg_ref[...], lb_ref[...])


def _merge(alpha, x, y_attn, pm, gates, w_a, w_p, w_o, ln_g, ln_b):
    b, s, d = x.shape
    tm = ROW_TILE
    row = lambda n: pl.BlockSpec((1, tm, n), lambda bi, i: (bi, i, 0))
    return pl.pallas_call(
        functools.partial(_merge_kernel, alpha),
        grid=(b, s // tm),
        in_specs=[row(d), row(D_ATTN), row(D_POOL), row(2 * D_MODEL),
                  _const_spec(w_a.shape), _const_spec(w_p.shape), _const_spec(w_o.shape),
                  _const_spec((1, d)), _const_spec((1, d))],
        out_specs=row(d),
        out_shape=jax.ShapeDtypeStruct((b, s, d), F32),
        compiler_params=pltpu.CompilerParams(
            dimension_semantics=("arbitrary", "arbitrary"), vmem_limit_bytes=VMEM_LIMIT),
        name="merge",
    )(x, y_attn, pm, gates, w_a, w_p, w_o, ln_g, ln_b)


def _ffn_kernel(alpha, x_ref, wg_ref, wu_ref, cw_ref, cb_ref, wd_ref, lg_ref, lb_ref, o_ref,
                abuf, carry):
    i = pl.program_id(1)
    tm = x_ref.shape[1]
    halo = SUBLANES

    @pl.when(i == 0)
    def _():
        carry[...] = jnp.zeros(carry.shape, F32)

    x = x_ref[0]
    xb = x.astype(BF16)
    acc = jnp.zeros((tm, D_MODEL), F32)
    for c in range(D_FF // FF_CHUNK):
        sl = slice(c * FF_CHUNK, (c + 1) * FF_CHUNK)
        a = _dot(xb, wg_ref[:, sl])
        u = _dot(xb, wu_ref[:, sl])
        abuf[0:halo, :] = carry[:, sl]
        abuf[halo:halo + tm, :] = a
        carry[:, sl] = abuf[tm:tm + halo, :]
        conv = a * cw_ref[CONV_WIDTH - 1:CONV_WIDTH, sl] + cb_ref[:, sl]
        for t in range(CONV_WIDTH - 1):
            back = CONV_WIDTH - 1 - t
            conv = conv + abuf[halo - back:halo - back + tm, :] * cw_ref[t:t + 1, sl]
        gelu = 0.5 * conv * (1.0 + lax.erf(conv * (2.0 ** -0.5)))
        acc = acc + _dot((gelu * u).astype(BF16), wd_ref[sl, :])
    o_ref[0] = _layer_norm(alpha * x + acc, lg_ref[...], lb_ref[...])


def _ffn(alpha, x, w_g, w_u, conv_w, conv_b, w_d, ln_g, ln_b):
    b, s, d = x.shape
    tm = ROW_TILE
    row = pl.BlockSpec((1, tm, d), lambda bi, i: (bi, i, 0))
    return pl.pallas_call(
        functools.partial(_ffn_kernel, alpha),
        grid=(b, s // tm),
        in_specs=[row, _const_spec(w_g.shape), _const_spec(w_u.shape),
                  _const_spec(conv_w.shape), _const_spec((1, D_FF)), _const_spec(w_d.shape),
                  _const_spec((1, d)), _const_spec((1, d))],
        out_specs=row,
        out_shape=jax.ShapeDtypeStruct((b, s, d), F32),
        scratch_shapes=[pltpu.VMEM((SUBLANES + tm, FF_CHUNK), F32),
                        pltpu.VMEM((SUBLANES, D_FF), F32)],
        compiler_params=pltpu.CompilerParams(
            dimension_semantics=("arbitrary", "arbitrary"), vmem_limit_bytes=VMEM_LIMIT),
        name="convffn",
    )(x, w_g, w_u, conv_w, conv_b, w_d, ln_g, ln_b)


def _rope_tables(s):
    half = HEAD_DIM // 2
    inv_freq = 1.0 / (ROPE_THETA ** (jnp.arange(half, dtype=F32) / half))
    ang = jnp.arange(s, dtype=F32)[:, None] * inv_freq[None, :]
    cos, sin = jnp.cos(ang), jnp.sin(ang)
    cos = jnp.tile(jnp.concatenate([cos, cos], axis=-1), (1, HEADS_PER_VREG))
    sin = jnp.tile(jnp.concatenate([-sin, sin], axis=-1), (1, HEADS_PER_VREG))
    return cos, sin


def kernel(x, w_in, b_gate, w_branch_attn, w_pool, pool_scale, w_branch_pool, w_out, ln1_g, ln1_b,
           w_ffn_gate, w_ffn_up, conv_w, conv_b, w_ffn_down, ln2_g, ln2_b):
    depth = w_in.shape[0]
    s = x.shape[1]
    alpha = (2.0 * depth) ** 0.25
    cos, sin = _rope_tables(s)
    vec = lambda t: t.reshape(1, -1)
    for l in range(depth):
        q, k, v, pm, gates = _inproj(x, w_in[l].astype(BF16), vec(b_gate[l]), cos, sin,
                                     w_pool[l].astype(BF16), vec(pool_scale[l]))
        y_attn = _moba(q, k, v)
        x = _merge(alpha, x, y_attn, pm, gates, w_branch_attn[l].astype(BF16),
                   w_branch_pool[l].astype(BF16), w_out[l].astype(BF16), vec(ln1_g[l]), vec(ln1_b[l]))
        x = _ffn(alpha, x, w_ffn_gate[l].astype(BF16), w_ffn_up[l].astype(BF16), conv_w[l],
                 vec(conv_b[l]), w_ffn_down[l].astype(BF16), vec(ln2_g[l]), vec(ln2_b[l]))
    return x
```

```python
import functools

import jax
import jax.numpy as jnp
from jax import lax
from jax.experimental import pallas as pl
from jax.experimental.pallas import tpu as pltpu

D_MODEL = 1024
ATTN_HEADS = 8
HEAD_DIM = 64
D_ATTN = ATTN_HEADS * HEAD_DIM
MOBA_BLOCK = 256
MOBA_TOPK = 3
ROPE_THETA = 10000.0
POOL_WINDOWS = (2, 4, 8, 16)
POOL_GROUP_DIM = 128
D_POOL = len(POOL_WINDOWS) * POOL_GROUP_DIM
D_FF = 2816
CONV_WIDTH = 3
LN_EPS = 1e-5
NEG = -1e30
Q_SCALE = HEAD_DIM ** -0.5 * 1.4426950408889634

LANES = 128
SUBLANES = 8
HEADS_PER_VREG = LANES // HEAD_DIM
POOL_HALO = 16
ROW_TILE = 512
FF_CHUNK = 256
LN_ROWS = 256
VMEM_LIMIT = 52 * 1024 * 1024

F32 = jnp.float32
BF16 = jnp.bfloat16


def _dot(a, b):
    return jnp.dot(a, b, preferred_element_type=F32)


def _dot_nt(a, b):
    return lax.dot_general(a, b, (((1,), (1,)), ((), ())), preferred_element_type=F32)


def _const_spec(shape):
    return pl.BlockSpec(shape, lambda *_: (0,) * len(shape), pipeline_mode=pl.Buffered(1))


def _layer_norm(y, g, b):
    mu = jnp.mean(y, axis=-1, keepdims=True)
    d = y - mu
    var = jnp.mean(d * d, axis=-1, keepdims=True)
    return d * lax.rsqrt(var + LN_EPS) * g + b


def _inproj_kernel(x_ref, w_ref, bg_ref, cos_ref, sin_ref, wpool_ref, pscale_ref,
                   q_ref, k_ref, v_ref, pm_ref, g_ref, ubuf):
    i = pl.program_id(1)
    tm = x_ref.shape[1]
    xb = x_ref[0].astype(BF16)

    lane = lax.broadcasted_iota(jnp.int32, (tm, LANES), 1)
    first_half = (lane % HEAD_DIM) < (HEAD_DIM // 2)
    cos = cos_ref[...]
    sin = sin_ref[...]

    def rope(t):
        partner = jnp.where(first_half, pltpu.roll(t, LANES - HEAD_DIM // 2, 1),
                            pltpu.roll(t, HEAD_DIM // 2, 1))
        return t * cos + partner * sin

    zq = _dot(xb, w_ref[:, 0:D_ATTN])
    for c in range(D_ATTN // LANES):
        sl = slice(c * LANES, (c + 1) * LANES)
        q_ref[0, :, sl] = (rope(zq[:, sl]) * Q_SCALE).astype(BF16)
    zk = _dot(xb, w_ref[:, D_ATTN:2 * D_ATTN])
    for c in range(D_ATTN // LANES):
        sl = slice(c * LANES, (c + 1) * LANES)
        k_ref[0, :, sl] = rope(zk[:, sl]).astype(BF16)
    v_ref[0] = _dot(xb, w_ref[:, 2 * D_ATTN:3 * D_ATTN]).astype(BF16)

    @pl.when(i == 0)
    def _():
        ubuf[0:POOL_HALO, :] = jnp.zeros((POOL_HALO, D_POOL), F32)

    o3 = 3 * D_ATTN
    ubuf[POOL_HALO:POOL_HALO + tm, :] = _dot(xb, w_ref[:, o3:o3 + D_POOL])
    pos = i * tm + lax.broadcasted_iota(jnp.int32, (tm, 1), 0)
    for g, w in enumerate(POOL_WINDOWS):
        sl = slice(g * POOL_GROUP_DIM, (g + 1) * POOL_GROUP_DIM)
        tok = ubuf[POOL_HALO:POOL_HALO + tm, sl]
        win = tok
        for d in range(1, w):
            win = win + ubuf[POOL_HALO - d:POOL_HALO - d + tm, sl]
        count = jnp.minimum(pos + 1, w).astype(F32)
        pooled = win / count - tok
        mixed = _dot(pooled.astype(BF16), wpool_ref[g]) * pscale_ref[:, sl]
        pm_ref[0, :, sl] = mixed.astype(BF16)
    ubuf[0:POOL_HALO, :] = ubuf[tm:tm + POOL_HALO, :]

    o4 = o3 + D_POOL
    for c in range(2 * D_MODEL // D_ATTN):
        sl = slice(c * D_ATTN, (c + 1) * D_ATTN)
        zg = _dot(xb, w_ref[:, o4 + c * D_ATTN:o4 + (c + 1) * D_ATTN]) + bg_ref[:, sl]
        g_ref[0, :, sl] = jax.nn.sigmoid(zg).astype(BF16)


def _inproj(x, w_in, b_gate, cos, sin, w_pool, pool_scale):
    b, s, d = x.shape
    tm = ROW_TILE
    n_in = w_in.shape[1]
    row = lambda n: pl.BlockSpec((1, tm, n), lambda bi, i: (bi, i, 0))
    return pl.pallas_call(
        _inproj_kernel,
        grid=(b, s // tm),
        in_specs=[
            row(d),
            _const_spec((d, n_in)),
            _const_spec((1, 2 * D_MODEL)),
            pl.BlockSpec((tm, LANES), lambda bi, i: (i, 0)),
            pl.BlockSpec((tm, LANES), lambda bi, i: (i, 0)),
            _const_spec(w_pool.shape),
            _const_spec((1, D_POOL)),
        ],
        out_specs=[row(D_ATTN), row(D_ATTN), row(D_ATTN), row(D_POOL), row(2 * D_MODEL)],
        out_shape=[
            jax.ShapeDtypeStruct((b, s, D_ATTN), BF16),
            jax.ShapeDtypeStruct((b, s, D_ATTN), BF16),
            jax.ShapeDtypeStruct((b, s, D_ATTN), BF16),
            jax.ShapeDtypeStruct((b, s, D_POOL), BF16),
            jax.ShapeDtypeStruct((b, s, 2 * D_MODEL), BF16),
        ],
        scratch_shapes=[pltpu.VMEM((POOL_HALO + tm, D_POOL), F32)],
        compiler_params=pltpu.CompilerParams(
            dimension_semantics=("arbitrary", "arbitrary"), vmem_limit_bytes=VMEM_LIMIT),
        name="inproj",
    )(x, w_in, b_gate, cos, sin, w_pool, pool_scale)


def _moba_kernel(q_ref, k_ref, v_ref, o_ref, vt_ref, s_ref):
    seq = k_ref.shape[1]
    nb = seq // MOBA_BLOCK
    blk = lambda j: slice(j * MOBA_BLOCK, (j + 1) * MOBA_BLOCK)

    kf = k_ref[0].astype(F32).reshape(nb, MOBA_BLOCK, LANES)
    km = jnp.sum(kf, axis=1) * (1.0 / MOBA_BLOCK)
    hi = km.astype(BF16)
    kmb = jnp.concatenate([hi, (km - hi.astype(F32)).astype(BF16)], axis=0)

    for j in range(nb):
        vt_ref[:, blk(j)] = v_ref[0, blk(j), :].astype(F32).T.astype(BF16)

    lane = lax.broadcasted_iota(jnp.int32, (MOBA_BLOCK, LANES), 1)
    key_i = lax.broadcasted_iota(jnp.int32, (MOBA_BLOCK, MOBA_BLOCK), 0)
    qry_i = lax.broadcasted_iota(jnp.int32, (MOBA_BLOCK, MOBA_BLOCK), 1)
    causal = key_i <= qry_i
    blk_i = lax.broadcasted_iota(jnp.int32, (nb, MOBA_BLOCK), 0)
    fold = lambda t: t.reshape(MOBA_BLOCK // SUBLANES, SUBLANES, MOBA_BLOCK)

    for i in range(nb):
        q = q_ref[0, blk(i), :]
        outs = []
        for h in range(HEADS_PER_VREG):
            qh = jnp.where((lane // HEAD_DIM) == h, q, jnp.zeros_like(q))
            if i > MOBA_TOPK:
                g2 = _dot_nt(kmb, qh)
                gate = g2[0:nb] + g2[nb:2 * nb]
                beaten = jnp.zeros((nb, MOBA_BLOCK), jnp.int32)
                for m in range(i):
                    gm = gate[m:m + 1, :]
                    wins = (gm > gate) | ((gm == gate) & (m < blk_i))
                    beaten = beaten + jnp.where(wins, 1, 0)
                sel = jnp.where(beaten < MOBA_TOPK, 1.0, 0.0)
            sbuf = s_ref.at[i % 2, h]
            m_run = jnp.full((SUBLANES, MOBA_BLOCK), NEG, F32)
            for j in range(i + 1):
                s = _dot_nt(k_ref[0, blk(j), :], qh)
                if j == i:
                    s = jnp.where(causal, s, NEG)
                elif i > MOBA_TOPK:
                    s = jnp.where(sel[j:j + 1, :] > 0.5, s, NEG)
                sbuf[blk(j), :] = s
                m_run = jnp.maximum(m_run, jnp.max(fold(s), axis=0))
            m_fin = jnp.max(m_run, axis=0, keepdims=True)
            l_run = jnp.zeros((SUBLANES, MOBA_BLOCK), F32)
            acc = jnp.zeros((HEAD_DIM, MOBA_BLOCK), F32)
            for j in range(i + 1):
                p = jnp.exp2(sbuf[blk(j), :] - m_fin)
                l_run = l_run + jnp.sum(fold(p), axis=0)
                acc = acc + _dot(vt_ref[h * HEAD_DIM:(h + 1) * HEAD_DIM, blk(j)], p.astype(BF16))
            outs.append(acc / jnp.sum(l_run, axis=0, keepdims=True))
        o_ref[0, blk(i), :] = jnp.concatenate(outs, axis=0).T.astype(BF16)


def _moba(q, k, v):
    b, s, _ = q.shape
    full = pl.BlockSpec((1, s, LANES), lambda bi, p: (bi, 0, p))
    return pl.pallas_call(
        _moba_kernel,
        grid=(b, D_ATTN // LANES),
        in_specs=[full, full, full],
        out_specs=full,
        out_shape=jax.ShapeDtypeStruct((b, s, D_ATTN), BF16),
        scratch_shapes=[pltpu.VMEM((LANES, s), BF16),
                        pltpu.VMEM((2, HEADS_PER_VREG, s, MOBA_BLOCK), F32)],
        compiler_params=pltpu.CompilerParams(
            dimension_semantics=("arbitrary", "arbitrary"), vmem_limit_bytes=VMEM_LIMIT),
        name="moba",
    )(q, k, v)


def _merge_kernel(alpha, x_ref, ya_ref, pm_ref, g_ref, wa_ref, wp_ref, wo_ref, lg_ref, lb_ref, o_ref):
    for r in range(x_ref.shape[1] // LN_ROWS):
        rows = slice(r * LN_ROWS, (r + 1) * LN_ROWS)
        y_attn = _dot(ya_ref[0, rows, :], wa_ref[...])
        y_pool = _dot(pm_ref[0, rows, :], wp_ref[...])
        g_attn = g_ref[0, rows, 0:D_MODEL].astype(F32)
        g_pool = g_ref[0, rows, D_MODEL:2 * D_MODEL].astype(F32)
        mix = _dot((g_attn * y_attn + g_pool * y_pool).astype(BF16), wo_ref[...])
        o_ref[0, rows, :] = _layer_norm(alpha * x_ref[0, rows, :] + mix, lg_ref[...], lb_ref[...])


def _merge(alpha, x, y_attn, pm, gates, w_a, w_p, w_o, ln_g, ln_b):
    b, s, d = x.shape
    tm = ROW_TILE
    row = lambda n: pl.BlockSpec((1, tm, n), lambda bi, i: (bi, i, 0))
    return pl.pallas_call(
        functools.partial(_merge_kernel, alpha),
        grid=(b, s // tm),
        in_specs=[row(d), row(D_ATTN), row(D_POOL), row(2 * D_MODEL),
                  _const_spec(w_a.shape), _const_spec(w_p.shape), _const_spec(w_o.shape),
                  _const_spec((1, d)), _const_spec((1, d))],
        out_specs=row(d),
        out_shape=jax.ShapeDtypeStruct((b, s, d), F32),
        compiler_params=pltpu.CompilerParams(
            dimension_semantics=("arbitrary", "arbitrary"), vmem_limit_bytes=VMEM_LIMIT),
        name="merge",
    )(x, y_attn, pm, gates, w_a, w_p, w_o, ln_g, ln_b)


def _ffn_kernel(alpha, x_ref, wg_ref, wu_ref, cw_ref, cb_ref, wd_ref, lg_ref, lb_ref, o_ref,
                abuf, hbuf):
    i = pl.program_id(1)
    tm = x_ref.shape[1]
    halo = SUBLANES

    @pl.when(i == 0)
    def _():
        abuf[0:halo, :] = jnp.zeros((halo, D_FF), F32)

    x = x_ref[0]
    xb = x.astype(BF16)
    for c in range(D_FF // FF_CHUNK):
        sl = slice(c * FF_CHUNK, (c + 1) * FF_CHUNK)
        a = _dot(xb, wg_ref[:, sl])
        u = _dot(xb, wu_ref[:, sl])
        abuf[halo:halo + tm, sl] = a
        conv = a * cw_ref[CONV_WIDTH - 1:CONV_WIDTH, sl] + cb_ref[:, sl]
        for t in range(CONV_WIDTH - 1):
            back = CONV_WIDTH - 1 - t
            conv = conv + abuf[halo - back:halo - back + tm, sl] * cw_ref[t:t + 1, sl]
        abuf[0:halo, sl] = a[tm - halo:tm, :]
        gelu = 0.5 * conv * (1.0 + lax.erf(conv * (2.0 ** -0.5)))
        hbuf[:, sl] = (gelu * u).astype(BF16)
    for r in range(tm // LN_ROWS):
        rows = slice(r * LN_ROWS, (r + 1) * LN_ROWS)
        ffn = _dot(hbuf[rows, :], wd_ref[...])
        o_ref[0, rows, :] = _layer_norm(alpha * x_ref[0, rows, :] + ffn, lg_ref[...], lb_ref[...])


def _ffn(alpha, x, w_g, w_u, conv_w, conv_b, w_d, ln_g, ln_b):
    b, s, d = x.shape
    tm = ROW_TILE
    row = pl.BlockSpec((1, tm, d), lambda bi, i: (bi, i, 0))
    return pl.pallas_call(
        functools.partial(_ffn_kernel, alpha),
        grid=(b, s // tm),
        in_specs=[row, _const_spec(w_g.shape), _const_spec(w_u.shape),
                  _const_spec(conv_w.shape), _const_spec((1, D_FF)), _const_spec(w_d.shape),
                  _const_spec((1, d)), _const_spec((1, d))],
        out_specs=row,
        out_shape=jax.ShapeDtypeStruct((b, s, d), F32),
        scratch_shapes=[pltpu.VMEM((SUBLANES + tm, D_FF), F32),
                        pltpu.VMEM((tm, D_FF), BF16)],
        compiler_params=pltpu.CompilerParams(
            dimension_semantics=("arbitrary", "arbitrary"), vmem_limit_bytes=VMEM_LIMIT),
        name="convffn",
    )(x, w_g, w_u, conv_w, conv_b, w_d, ln_g, ln_b)


def _rope_tables(s):
    half = HEAD_DIM // 2
    inv_freq = 1.0 / (ROPE_THETA ** (jnp.arange(half, dtype=F32) / half))
    ang = jnp.arange(s, dtype=F32)[:, None] * inv_freq[None, :]
    cos, sin = jnp.cos(ang), jnp.sin(ang)
    cos = jnp.tile(jnp.concatenate([cos, cos], axis=-1), (1, HEADS_PER_VREG))
    sin = jnp.tile(jnp.concatenate([-sin, sin], axis=-1), (1, HEADS_PER_VREG))
    return cos, sin


def kernel(x, w_in, b_gate, w_branch_attn, w_pool, pool_scale, w_branch_pool, w_out, ln1_g, ln1_b,
           w_ffn_gate, w_ffn_up, conv_w, conv_b, w_ffn_down, ln2_g, ln2_b):
    depth = w_in.shape[0]
    s = x.shape[1]
    alpha = (2.0 * depth) ** 0.25
    cos, sin = _rope_tables(s)
    vec = lambda t: t.reshape(1, -1)
    for l in range(depth):
        q, k, v, pm, gates = _inproj(x, w_in[l].astype(BF16), vec(b_gate[l]), cos, sin,
                                     w_pool[l].astype(BF16), vec(pool_scale[l]))
        y_attn = _moba(q, k, v)
        x = _merge(alpha, x, y_attn, pm, gates, w_branch_attn[l].astype(BF16),
                   w_branch_pool[l].astype(BF16), w_out[l].astype(BF16), vec(ln1_g[l]), vec(ln1_b[l]))
        x = _ffn(alpha, x, w_ffn_gate[l].astype(BF16), w_ffn_up[l].astype(BF16), conv_w[l],
                 vec(conv_b[l]), w_ffn_down[l].astype(BF16), vec(ln2_g[l]), vec(ln2_b[l]))
    return x
```

```python
import functools

import jax
import jax.numpy as jnp
from jax import lax
from jax.experimental import pallas as pl
from jax.experimental.pallas import tpu as pltpu

D_MODEL = 1024
ATTN_HEADS = 8
HEAD_DIM = 64
D_ATTN = ATTN_HEADS * HEAD_DIM
MOBA_BLOCK = 256
MOBA_TOPK = 3
ROPE_THETA = 10000.0
POOL_WINDOWS = (2, 4, 8, 16)
POOL_GROUP_DIM = 128
D_POOL = len(POOL_WINDOWS) * POOL_GROUP_DIM
D_FF = 2816
CONV_WIDTH = 3
LN_EPS = 1e-5
NEG = -1e30
Q_SCALE = HEAD_DIM ** -0.5 * 1.4426950408889634

LANES = 128
SUBLANES = 8
HEADS_PER_VREG = LANES // HEAD_DIM
BF16_ROWS = 2 * SUBLANES
PV_ROWS = HEAD_DIM + BF16_ROWS
SCORE_BUFFERS = 3
POOL_HALO = 16
ROW_TILE = 512
FF_CHUNK = 256
LN_ROWS = 256
VMEM_LIMIT = 52 * 1024 * 1024

F32 = jnp.float32
BF16 = jnp.bfloat16


def _dot(a, b):
    return jnp.dot(a, b, preferred_element_type=F32)


def _dot_nt(a, b):
    return lax.dot_general(a, b, (((1,), (1,)), ((), ())), preferred_element_type=F32)


def _const_spec(shape):
    return pl.BlockSpec(shape, lambda *_: (0,) * len(shape), pipeline_mode=pl.Buffered(1))


def _layer_norm(y, g, b):
    mu = jnp.mean(y, axis=-1, keepdims=True)
    d = y - mu
    var = jnp.mean(d * d, axis=-1, keepdims=True)
    return d * lax.rsqrt(var + LN_EPS) * g + b


def _inproj_kernel(x_ref, w_ref, bg_ref, cos_ref, sin_ref, wpool_ref, pscale_ref,
                   q_ref, k_ref, v_ref, pm_ref, g_ref, ubuf):
    i = pl.program_id(1)
    tm = x_ref.shape[1]
    xb = x_ref[0].astype(BF16)

    lane = lax.broadcasted_iota(jnp.int32, (tm, LANES), 1)
    first_half = (lane % HEAD_DIM) < (HEAD_DIM // 2)
    cos = cos_ref[...]
    sin = sin_ref[...]

    def rope(t):
        partner = jnp.where(first_half, pltpu.roll(t, LANES - HEAD_DIM // 2, 1),
                            pltpu.roll(t, HEAD_DIM // 2, 1))
        return t * cos + partner * sin

    zq = _dot(xb, w_ref[:, 0:D_ATTN])
    for c in range(D_ATTN // LANES):
        sl = slice(c * LANES, (c + 1) * LANES)
        q_ref[0, :, sl] = (rope(zq[:, sl]) * Q_SCALE).astype(BF16)
    zk = _dot(xb, w_ref[:, D_ATTN:2 * D_ATTN])
    for c in range(D_ATTN // LANES):
        sl = slice(c * LANES, (c + 1) * LANES)
        k_ref[0, :, sl] = rope(zk[:, sl]).astype(BF16)
    v_ref[0] = _dot(xb, w_ref[:, 2 * D_ATTN:3 * D_ATTN]).astype(BF16)

    @pl.when(i == 0)
    def _():
        ubuf[0:POOL_HALO, :] = jnp.zeros((POOL_HALO, D_POOL), F32)

    o3 = 3 * D_ATTN
    ubuf[POOL_HALO:POOL_HALO + tm, :] = _dot(xb, w_ref[:, o3:o3 + D_POOL])
    pos = i * tm + lax.broadcasted_iota(jnp.int32, (tm, 1), 0)
    for g, w in enumerate(POOL_WINDOWS):
        sl = slice(g * POOL_GROUP_DIM, (g + 1) * POOL_GROUP_DIM)
        tok = ubuf[POOL_HALO:POOL_HALO + tm, sl]
        win = tok
        for d in range(1, w):
            win = win + ubuf[POOL_HALO - d:POOL_HALO - d + tm, sl]
        count = jnp.minimum(pos + 1, w).astype(F32)
        pooled = win / count - tok
        mixed = _dot(pooled.astype(BF16), wpool_ref[g]) * pscale_ref[:, sl]
        pm_ref[0, :, sl] = mixed.astype(BF16)
    ubuf[0:POOL_HALO, :] = ubuf[tm:tm + POOL_HALO, :]

    o4 = o3 + D_POOL
    for c in range(2 * D_MODEL // D_ATTN):
        sl = slice(c * D_ATTN, (c + 1) * D_ATTN)
        zg = _dot(xb, w_ref[:, o4 + c * D_ATTN:o4 + (c + 1) * D_ATTN]) + bg_ref[:, sl]
        g_ref[0, :, sl] = jax.nn.sigmoid(zg).astype(BF16)


def _inproj(x, w_in, b_gate, cos, sin, w_pool, pool_scale):
    b, s, d = x.shape
    tm = ROW_TILE
    n_in = w_in.shape[1]
    row = lambda n: pl.BlockSpec((1, tm, n), lambda bi, i: (bi, i, 0))
    return pl.pallas_call(
        _inproj_kernel,
        grid=(b, s // tm),
        in_specs=[
            row(d),
            _const_spec((d, n_in)),
            _const_spec((1, 2 * D_MODEL)),
            pl.BlockSpec((tm, LANES), lambda bi, i: (i, 0)),
            pl.BlockSpec((tm, LANES), lambda bi, i: (i, 0)),
            _const_spec(w_pool.shape),
            _const_spec((1, D_POOL)),
        ],
        out_specs=[row(D_ATTN), row(D_ATTN), row(D_ATTN), row(D_POOL), row(2 * D_MODEL)],
        out_shape=[
            jax.ShapeDtypeStruct((b, s, D_ATTN), BF16),
            jax.ShapeDtypeStruct((b, s, D_ATTN), BF16),
            jax.ShapeDtypeStruct((b, s, D_ATTN), BF16),
            jax.ShapeDtypeStruct((b, s, D_POOL), BF16),
            jax.ShapeDtypeStruct((b, s, 2 * D_MODEL), BF16),
        ],
        scratch_shapes=[pltpu.VMEM((POOL_HALO + tm, D_POOL), F32)],
        compiler_params=pltpu.CompilerParams(
            dimension_semantics=("arbitrary", "arbitrary"), vmem_limit_bytes=VMEM_LIMIT),
        name="inproj",
    )(x, w_in, b_gate, cos, sin, w_pool, pool_scale)


def _moba_kernel(q_ref, k_ref, v_ref, o_ref, khat_ref, vt_ref, *s_refs):
    seq = k_ref.shape[1]
    nb = seq // MOBA_BLOCK
    blk = lambda j: slice(j * MOBA_BLOCK, (j + 1) * MOBA_BLOCK)
    lane = lax.broadcasted_iota(jnp.int32, (MOBA_BLOCK, LANES), 1)

    kf = k_ref[0].astype(F32).reshape(nb, MOBA_BLOCK, LANES)
    km = jnp.sum(kf, axis=1) * (1.0 / MOBA_BLOCK)
    hi = km.astype(BF16)
    kmb = jnp.concatenate([hi, (km - hi.astype(F32)).astype(BF16)], axis=0)

    for j in range(nb):
        kj = k_ref[0, blk(j), :]
        for h in range(HEADS_PER_VREG):
            onehot = jnp.where(lane == (1 - h) * HEAD_DIM + j, 1.0, 0.0).astype(BF16)
            khat_ref[h, blk(j), :] = jnp.where((lane // HEAD_DIM) == h, kj, onehot)
        vt = v_ref[0, blk(j), :].astype(F32).T.astype(BF16)
        for h in range(HEADS_PER_VREG):
            vt_ref[h, 0:HEAD_DIM, blk(j)] = vt[h * HEAD_DIM:(h + 1) * HEAD_DIM]
            vt_ref[h, HEAD_DIM:PV_ROWS, blk(j)] = jnp.ones((PV_ROWS - HEAD_DIM, MOBA_BLOCK), BF16)

    key_i = lax.broadcasted_iota(jnp.int32, (MOBA_BLOCK, MOBA_BLOCK), 0)
    qry_i = lax.broadcasted_iota(jnp.int32, (MOBA_BLOCK, MOBA_BLOCK), 1)
    causal = key_i <= qry_i
    blk_i = lax.broadcasted_iota(jnp.int32, (nb, MOBA_BLOCK), 0)
    fold = lambda t: t.reshape(MOBA_BLOCK // SUBLANES, SUBLANES, MOBA_BLOCK)

    units = [(i, h) for i in range(nb) for h in range(HEADS_PER_VREG)]
    state = {}
    sbuf = lambda u: s_refs[units.index(u) % len(s_refs)]

    def begin(u):
        i, h = u
        in_head = (lane // HEAD_DIM) == h
        q = q_ref[0, blk(i), :]
        if i > MOBA_TOPK:
            g2 = _dot_nt(kmb, jnp.where(in_head, q, jnp.zeros_like(q)))
            gate = g2[0:nb] + g2[nb:2 * nb]
            beaten = jnp.zeros((nb, MOBA_BLOCK), jnp.int32)
            for m in range(i):
                gm = gate[m:m + 1, :]
                wins = (gm > gate) | ((gm == gate) & (m < blk_i))
                beaten = beaten + jnp.where(wins, 1, 0)
            bias = jnp.where((blk_i < i) & (beaten >= MOBA_TOPK), NEG, 0.0)
            base = (1 - h) * HEAD_DIM
            parts = [bias, jnp.zeros((LANES - base - nb, MOBA_BLOCK), F32)]
            if base:
                parts = [jnp.zeros((base, MOBA_BLOCK), F32)] + parts
            spare = jnp.concatenate(parts, axis=0).T.astype(BF16)
        else:
            spare = jnp.zeros_like(q)
        state[u] = dict(m_run=jnp.full((SUBLANES, MOBA_BLOCK), NEG, F32),
                        acc=jnp.zeros((PV_ROWS, MOBA_BLOCK), F32))
        keys = slice(0, (i + 1) * MOBA_BLOCK)
        sbuf(u)[keys, :] = _dot_nt(khat_ref[h, keys, :], jnp.where(in_head, q, spare))

    def masked_scores(u, j):
        s = sbuf(u)[blk(j), :]
        return jnp.where(causal, s, NEG) if j == u[0] else s

    def score_step(u, j):
        st = state[u]
        st["m_run"] = jnp.maximum(st["m_run"], jnp.max(fold(masked_scores(u, j)), axis=0))

    def end_scores(u):
        state[u]["m_fin"] = jnp.max(state[u]["m_run"], axis=0, keepdims=True)

    def value_step(u, j):
        st = state[u]
        p = jnp.exp2(masked_scores(u, j) - st["m_fin"])
        st["acc"] = st["acc"] + _dot(vt_ref[u[1], :, blk(j)], p.astype(BF16))

    def finish(u):
        i, h = u
        acc = state.pop(u)["acc"]
        out = acc[0:HEAD_DIM] / acc[HEAD_DIM:HEAD_DIM + 1]
        if h == 0:
            state["head0"] = out
        else:
            both = jnp.concatenate([state.pop("head0"), out], axis=0)
            o_ref[0, blk(i), :] = both.T.astype(BF16)

    def stage_tasks(t):
        unit = lambda d: units[t + d] if 0 <= t + d < len(units) else None
        stages = []
        if unit(2):
            stages.append([functools.partial(begin, unit(2))])
        if unit(1):
            stages.append([functools.partial(score_step, unit(1), j) for j in range(unit(1)[0] + 1)]
                          + [functools.partial(end_scores, unit(1))])
        if unit(0):
            stages.append([functools.partial(value_step, unit(0), j) for j in range(unit(0)[0] + 1)])
        return stages

    for t in range(-2, len(units)):
        stages = stage_tasks(t)
        n = max(len(tasks) for tasks in stages)
        for step in range(n):
            for tasks in stages:
                lo, hi_ = step * len(tasks) // n, (step + 1) * len(tasks) // n
                for task in tasks[lo:hi_]:
                    task()
        if t >= 0:
            finish(units[t])


def _moba(q, k, v):
    b, s, _ = q.shape
    full = pl.BlockSpec((1, s, LANES), lambda bi, p: (bi, 0, p))
    return pl.pallas_call(
        _moba_kernel,
        grid=(b, D_ATTN // LANES),
        in_specs=[full, full, full],
        out_specs=full,
        out_shape=jax.ShapeDtypeStruct((b, s, D_ATTN), BF16),
        scratch_shapes=[pltpu.VMEM((HEADS_PER_VREG, s, LANES), BF16),
                        pltpu.VMEM((HEADS_PER_VREG, PV_ROWS, s), BF16),
                        *[pltpu.VMEM((s, MOBA_BLOCK), F32)] * SCORE_BUFFERS],
        compiler_params=pltpu.CompilerParams(
            dimension_semantics=("arbitrary", "arbitrary"), vmem_limit_bytes=VMEM_LIMIT),
        name="moba",
    )(q, k, v)


def _merge_kernel(alpha, x_ref, ya_ref, pm_ref, g_ref, wa_ref, wp_ref, wo_ref, lg_ref, lb_ref, o_ref):
    for r in range(x_ref.shape[1] // LN_ROWS):
        rows = slice(r * LN_ROWS, (r + 1) * LN_ROWS)
        y_attn = _dot(ya_ref[0, rows, :], wa_ref[...])
        y_pool = _dot(pm_ref[0, rows, :], wp_ref[...])
        g_attn = g_ref[0, rows, 0:D_MODEL].astype(F32)
        g_pool = g_ref[0, rows, D_MODEL:2 * D_MODEL].astype(F32)
        mix = _dot((g_attn * y_attn + g_pool * y_pool).astype(BF16), wo_ref[...])
        o_ref[0, rows, :] = _layer_norm(alpha * x_ref[0, rows, :] + mix, lg_ref[...], lb_ref[...])


def _merge(alpha, x, y_attn, pm, gates, w_a, w_p, w_o, ln_g, ln_b):
    b, s, d = x.shape
    tm = ROW_TILE
    row = lambda n: pl.BlockSpec((1, tm, n), lambda bi, i: (bi, i, 0))
    return pl.pallas_call(
        functools.partial(_merge_kernel, alpha),
        grid=(b, s // tm),
        in_specs=[row(d), row(D_ATTN), row(D_POOL), row(2 * D_MODEL),
                  _const_spec(w_a.shape), _const_spec(w_p.shape), _const_spec(w_o.shape),
                  _const_spec((1, d)), _const_spec((1, d))],
        out_specs=row(d),
        out_shape=jax.ShapeDtypeStruct((b, s, d), F32),
        compiler_params=pltpu.CompilerParams(
            dimension_semantics=("arbitrary", "arbitrary"), vmem_limit_bytes=VMEM_LIMIT),
        name="merge",
    )(x, y_attn, pm, gates, w_a, w_p, w_o, ln_g, ln_b)


def _ffn_kernel(alpha, x_ref, wg_ref, wu_ref, cw_ref, cb_ref, wd_ref, lg_ref, lb_ref, o_ref,
                abuf, hbuf):
    i = pl.program_id(1)
    tm = x_ref.shape[1]
    halo = SUBLANES

    @pl.when(i == 0)
    def _():
        abuf[0:halo, :] = jnp.zeros((halo, D_FF), F32)

    x = x_ref[0]
    xb = x.astype(BF16)
    for c in range(D_FF // FF_CHUNK):
        sl = slice(c * FF_CHUNK, (c + 1) * FF_CHUNK)
        a = _dot(xb, wg_ref[:, sl])
        u = _dot(xb, wu_ref[:, sl])
        abuf[halo:halo + tm, sl] = a
        conv = a * cw_ref[CONV_WIDTH - 1:CONV_WIDTH, sl] + cb_ref[:, sl]
        for t in range(CONV_WIDTH - 1):
            back = CONV_WIDTH - 1 - t
            conv = conv + abuf[halo - back:halo - back + tm, sl] * cw_ref[t:t + 1, sl]
        abuf[0:halo, sl] = a[tm - halo:tm, :]
        gelu = 0.5 * conv * (1.0 + lax.erf(conv * (2.0 ** -0.5)))
        hbuf[:, sl] = (gelu * u).astype(BF16)
    for r in range(tm // LN_ROWS):
        rows = slice(r * LN_ROWS, (r + 1) * LN_ROWS)
        ffn = _dot(hbuf[rows, :], wd_ref[...])
        o_ref[0, rows, :] = _layer_norm(alpha * x_ref[0, rows, :] + ffn, lg_ref[...], lb_ref[...])


def _ffn(alpha, x, w_g, w_u, conv_w, conv_b, w_d, ln_g, ln_b):
    b, s, d = x.shape
    tm = ROW_TILE
    row = pl.BlockSpec((1, tm, d), lambda bi, i: (bi, i, 0))
    return pl.pallas_call(
        functools.partial(_ffn_kernel, alpha),
        grid=(b, s // tm),
        in_specs=[row, _const_spec(w_g.shape), _const_spec(w_u.shape),
                  _const_spec(conv_w.shape), _const_spec((1, D_FF)), _const_spec(w_d.shape),
                  _const_spec((1, d)), _const_spec((1, d))],
        out_specs=row,
        out_shape=jax.ShapeDtypeStruct((b, s, d), F32),
        scratch_shapes=[pltpu.VMEM((SUBLANES + tm, D_FF), F32),
                        pltpu.VMEM((tm, D_FF), BF16)],
        compiler_params=pltpu.CompilerParams(
            dimension_semantics=("arbitrary", "arbitrary"), vmem_limit_bytes=VMEM_LIMIT),
        name="convffn",
    )(x, w_g, w_u, conv_w, conv_b, w_d, ln_g, ln_b)


def _rope_tables(s):
    half = HEAD_DIM // 2
    inv_freq = 1.0 / (ROPE_THETA ** (jnp.arange(half, dtype=F32) / half))
    ang = jnp.arange(s, dtype=F32)[:, None] * inv_freq[None, :]
    cos, sin = jnp.cos(ang), jnp.sin(ang)
    cos = jnp.tile(jnp.concatenate([cos, cos], axis=-1), (1, HEADS_PER_VREG))
    sin = jnp.tile(jnp.concatenate([-sin, sin], axis=-1), (1, HEADS_PER_VREG))
    return cos, sin


def kernel(x, w_in, b_gate, w_branch_attn, w_pool, pool_scale, w_branch_pool, w_out, ln1_g, ln1_b,
           w_ffn_gate, w_ffn_up, conv_w, conv_b, w_ffn_down, ln2_g, ln2_b):
    depth = w_in.shape[0]
    s = x.shape[1]
    alpha = (2.0 * depth) ** 0.25
    cos, sin = _rope_tables(s)
    vec = lambda t: t.reshape(1, -1)
    for l in range(depth):
        q, k, v, pm, gates = _inproj(x, w_in[l].astype(BF16), vec(b_gate[l]), cos, sin,
                                     w_pool[l].astype(BF16), vec(pool_scale[l]))
        y_attn = _moba(q, k, v)
        x = _merge(alpha, x, y_attn, pm, gates, w_branch_attn[l].astype(BF16),
                   w_branch_pool[l].astype(BF16), w_out[l].astype(BF16), vec(ln1_g[l]), vec(ln1_b[l]))
        x = _ffn(alpha, x, w_ffn_gate[l].astype(BF16), w_ffn_up[l].astype(BF16), conv_w[l],
                 vec(conv_b[l]), w_ffn_down[l].astype(BF16), vec(ln2_g[l]), vec(ln2_b[l]))
    return x
```

```python
import functools

import jax
import jax.numpy as jnp
from jax import lax
from jax.experimental import pallas as pl
from jax.experimental.pallas import tpu as pltpu

D_MODEL = 1024
ATTN_HEADS = 8
HEAD_DIM = 64
D_ATTN = ATTN_HEADS * HEAD_DIM
MOBA_BLOCK = 256
MOBA_TOPK = 3
ROPE_THETA = 10000.0
POOL_WINDOWS = (2, 4, 8, 16)
POOL_GROUP_DIM = 128
D_POOL = len(POOL_WINDOWS) * POOL_GROUP_DIM
D_FF = 2816
CONV_WIDTH = 3
LN_EPS = 1e-5
NEG = -1e30
Q_SCALE = HEAD_DIM ** -0.5 * 1.4426950408889634

LANES = 128
SUBLANES = 8
HEADS_PER_VREG = LANES // HEAD_DIM
BF16_ROWS = 2 * SUBLANES
PV_ROWS = HEAD_DIM + BF16_ROWS
SCORE_BUFFERS = 3
POOL_HALO = 16
ROW_TILE = 512
FF_CHUNK = 256
LN_ROWS = 256
VMEM_LIMIT = 52 * 1024 * 1024

F32 = jnp.float32
BF16 = jnp.bfloat16


def _dot(a, b):
    return jnp.dot(a, b, preferred_element_type=F32)


def _dot_nt(a, b):
    return lax.dot_general(a, b, (((1,), (1,)), ((), ())), preferred_element_type=F32)


def _const_spec(shape):
    return pl.BlockSpec(shape, lambda *_: (0,) * len(shape), pipeline_mode=pl.Buffered(1))


def _layer_norm(y, g, b):
    mu = jnp.mean(y, axis=-1, keepdims=True)
    d = y - mu
    var = jnp.mean(d * d, axis=-1, keepdims=True)
    return d * lax.rsqrt(var + LN_EPS) * g + b


def _inproj_kernel(x_ref, w_ref, bg_ref, cos_ref, sin_ref, wpool_ref, pscale_ref,
                   q_ref, k_ref, v_ref, pm_ref, g_ref, ubuf, *lvl):
    i = pl.program_id(1)
    tm = x_ref.shape[1]
    xb = x_ref[0].astype(BF16)

    lane = lax.broadcasted_iota(jnp.int32, (tm, LANES), 1)
    first_half = (lane % HEAD_DIM) < (HEAD_DIM // 2)
    cos = cos_ref[...]
    sin = sin_ref[...]

    def rope(t):
        partner = jnp.where(first_half, pltpu.roll(t, LANES - HEAD_DIM // 2, 1),
                            pltpu.roll(t, HEAD_DIM // 2, 1))
        return t * cos + partner * sin

    pad, cur, end = SUBLANES, SUBLANES + POOL_HALO, SUBLANES + POOL_HALO + tm

    @pl.when(i == 0)
    def _():
        ubuf[0:cur, :] = jnp.zeros((cur, D_POOL), F32)
        for level in lvl:
            level[0:pad, :] = jnp.zeros((pad, level.shape[1]), F32)

    o3 = 3 * D_ATTN
    ubuf[cur:end, :] = _dot(xb, w_ref[:, o3:o3 + D_POOL])

    zq = _dot(xb, w_ref[:, 0:D_ATTN])
    for c in range(D_ATTN // LANES):
        sl = slice(c * LANES, (c + 1) * LANES)
        q_ref[0, :, sl] = (rope(zq[:, sl]) * Q_SCALE).astype(BF16)
    zk = _dot(xb, w_ref[:, D_ATTN:2 * D_ATTN])
    for c in range(D_ATTN // LANES):
        sl = slice(c * LANES, (c + 1) * LANES)
        k_ref[0, :, sl] = rope(zk[:, sl]).astype(BF16)
    v_ref[0] = _dot(xb, w_ref[:, 2 * D_ATTN:3 * D_ATTN]).astype(BF16)

    o4 = o3 + D_POOL
    for c in range(2 * D_MODEL // D_ATTN):
        sl = slice(c * D_ATTN, (c + 1) * D_ATTN)
        zg = _dot(xb, w_ref[:, o4 + c * D_ATTN:o4 + (c + 1) * D_ATTN]) + bg_ref[:, sl]
        g_ref[0, :, sl] = jax.nn.sigmoid(zg).astype(BF16)

    pos = i * tm + lax.broadcasted_iota(jnp.int32, (tm, 1), 0)
    prev = ubuf
    for g, w in enumerate(POOL_WINDOWS):
        sl = slice(g * POOL_GROUP_DIM, (g + 1) * POOL_GROUP_DIM)
        sums = prev[pad:end, :] + prev[pad - w // 2:end - w // 2, :]
        if g + 1 < len(POOL_WINDOWS):
            lvl[g][pad:end, :] = sums[:, POOL_GROUP_DIM:]
            prev = lvl[g]
        inv_count = 1.0 / jnp.minimum(pos + 1, w).astype(F32)
        pooled = sums[POOL_HALO:, 0:POOL_GROUP_DIM] * inv_count - ubuf[cur:end, sl]
        mixed = _dot(pooled.astype(BF16), wpool_ref[g]) * pscale_ref[:, sl]
        pm_ref[0, :, sl] = mixed.astype(BF16)
    ubuf[pad:cur, :] = ubuf[end - POOL_HALO:end, :]


def _inproj(x, w_in, b_gate, cos, sin, w_pool, pool_scale):
    b, s, d = x.shape
    tm = ROW_TILE
    n_in = w_in.shape[1]
    assert all(w == 2 ** (g + 1) for g, w in enumerate(POOL_WINDOWS)) and POOL_WINDOWS[-1] <= POOL_HALO
    row = lambda n: pl.BlockSpec((1, tm, n), lambda bi, i: (bi, i, 0))
    return pl.pallas_call(
        _inproj_kernel,
        grid=(b, s // tm),
        in_specs=[
            row(d),
            _const_spec((d, n_in)),
            _const_spec((1, 2 * D_MODEL)),
            pl.BlockSpec((tm, LANES), lambda bi, i: (i, 0)),
            pl.BlockSpec((tm, LANES), lambda bi, i: (i, 0)),
            _const_spec(w_pool.shape),
            _const_spec((1, D_POOL)),
        ],
        out_specs=[row(D_ATTN), row(D_ATTN), row(D_ATTN), row(D_POOL), row(2 * D_MODEL)],
        out_shape=[
            jax.ShapeDtypeStruct((b, s, D_ATTN), BF16),
            jax.ShapeDtypeStruct((b, s, D_ATTN), BF16),
            jax.ShapeDtypeStruct((b, s, D_ATTN), BF16),
            jax.ShapeDtypeStruct((b, s, D_POOL), BF16),
            jax.ShapeDtypeStruct((b, s, 2 * D_MODEL), BF16),
        ],
        scratch_shapes=[pltpu.VMEM((SUBLANES + POOL_HALO + tm, D_POOL - g * POOL_GROUP_DIM), F32)
                        for g in range(len(POOL_WINDOWS))],
        compiler_params=pltpu.CompilerParams(
            dimension_semantics=("arbitrary", "arbitrary"), vmem_limit_bytes=VMEM_LIMIT),
        name="inproj",
    )(x, w_in, b_gate, cos, sin, w_pool, pool_scale)


def _moba_kernel(q_ref, k_ref, v_ref, o_ref, khat_ref, vt_ref, *s_refs):
    seq = k_ref.shape[1]
    nb = seq // MOBA_BLOCK
    blk = lambda j: slice(j * MOBA_BLOCK, (j + 1) * MOBA_BLOCK)
    lane = lax.broadcasted_iota(jnp.int32, (MOBA_BLOCK, LANES), 1)

    kf = k_ref[0].astype(F32).reshape(nb, MOBA_BLOCK, LANES)
    km = jnp.sum(kf, axis=1) * (1.0 / MOBA_BLOCK)
    hi = km.astype(BF16)
    kmb = jnp.concatenate([hi, (km - hi.astype(F32)).astype(BF16)], axis=0)

    for j in range(nb):
        kj = k_ref[0, blk(j), :]
        for h in range(HEADS_PER_VREG):
            onehot = jnp.where(lane == (1 - h) * HEAD_DIM + j, 1.0, 0.0).astype(BF16)
            khat_ref[h, blk(j), :] = jnp.where((lane // HEAD_DIM) == h, kj, onehot)
        vt = v_ref[0, blk(j), :].astype(F32).T.astype(BF16)
        for h in range(HEADS_PER_VREG):
            vt_ref[h, 0:HEAD_DIM, blk(j)] = vt[h * HEAD_DIM:(h + 1) * HEAD_DIM]
            vt_ref[h, HEAD_DIM:PV_ROWS, blk(j)] = jnp.ones((PV_ROWS - HEAD_DIM, MOBA_BLOCK), BF16)

    key_i = lax.broadcasted_iota(jnp.int32, (MOBA_BLOCK, MOBA_BLOCK), 0)
    qry_i = lax.broadcasted_iota(jnp.int32, (MOBA_BLOCK, MOBA_BLOCK), 1)
    causal = key_i <= qry_i
    blk_i = lax.broadcasted_iota(jnp.int32, (nb, MOBA_BLOCK), 0)
    fold = lambda t: t.reshape(MOBA_BLOCK // SUBLANES, SUBLANES, MOBA_BLOCK)

    units = [(i, h) for i in range(nb) for h in range(HEADS_PER_VREG)]
    state = {}
    sbuf = lambda u: s_refs[units.index(u) % len(s_refs)]

    def begin(u):
        i, h = u
        in_head = (lane // HEAD_DIM) == h
        q = q_ref[0, blk(i), :]
        if i > MOBA_TOPK:
            g2 = _dot_nt(kmb, jnp.where(in_head, q, jnp.zeros_like(q)))
            gate = g2[0:nb] + g2[nb:2 * nb]
            beaten = jnp.zeros((nb, MOBA_BLOCK), jnp.int32)
            for m in range(i):
                gm = gate[m:m + 1, :]
                wins = (gm > gate) | ((gm == gate) & (m < blk_i))
                beaten = beaten + jnp.where(wins, 1, 0)
            bias = jnp.where((blk_i < i) & (beaten >= MOBA_TOPK), NEG, 0.0)
            base = (1 - h) * HEAD_DIM
            parts = [bias, jnp.zeros((LANES - base - nb, MOBA_BLOCK), F32)]
            if base:
                parts = [jnp.zeros((base, MOBA_BLOCK), F32)] + parts
            spare = jnp.concatenate(parts, axis=0).T.astype(BF16)
        else:
            spare = jnp.zeros_like(q)
        state[u] = dict(m_run=jnp.full((SUBLANES, MOBA_BLOCK), NEG, F32),
                        acc=jnp.zeros((PV_ROWS, MOBA_BLOCK), F32))
        keys = slice(0, (i + 1) * MOBA_BLOCK)
        sbuf(u)[keys, :] = _dot_nt(khat_ref[h, keys, :], jnp.where(in_head, q, spare))

    def masked_scores(u, j):
        s = sbuf(u)[blk(j), :]
        return jnp.where(causal, s, NEG) if j == u[0] else s

    def score_step(u, j):
        st = state[u]
        st["m_run"] = jnp.maximum(st["m_run"], jnp.max(fold(masked_scores(u, j)), axis=0))

    def end_scores(u):
        state[u]["m_fin"] = jnp.max(state[u]["m_run"], axis=0, keepdims=True)

    def value_step(u, j):
        st = state[u]
        p = jnp.exp2(masked_scores(u, j) - st["m_fin"])
        st["acc"] = st["acc"] + _dot(vt_ref[u[1], :, blk(j)], p.astype(BF16))

    def finish(u):
        i, h = u
        acc = state.pop(u)["acc"]
        out = acc[0:HEAD_DIM] / acc[HEAD_DIM:HEAD_DIM + 1]
        if h == 0:
            state["head0"] = out
        else:
            both = jnp.concatenate([state.pop("head0"), out], axis=0)
            o_ref[0, blk(i), :] = both.T.astype(BF16)

    def stage_tasks(t):
        unit = lambda d: units[t + d] if 0 <= t + d < len(units) else None
        stages = []
        if unit(2):
            stages.append([functools.partial(begin, unit(2))])
        if unit(1):
            stages.append([functools.partial(score_step, unit(1), j) for j in range(unit(1)[0] + 1)]
                          + [functools.partial(end_scores, unit(1))])
        if unit(0):
            stages.append([functools.partial(value_step, unit(0), j) for j in range(unit(0)[0] + 1)])
        return stages

    for t in range(-2, len(units)):
        stages = stage_tasks(t)
        n = max(len(tasks) for tasks in stages)
        for step in range(n):
            for tasks in stages:
                lo, hi_ = step * len(tasks) // n, (step + 1) * len(tasks) // n
                for task in tasks[lo:hi_]:
                    task()
        if t >= 0:
            finish(units[t])


def _moba(q, k, v):
    b, s, _ = q.shape
    full = pl.BlockSpec((1, s, LANES), lambda bi, p: (bi, 0, p))
    return pl.pallas_call(
        _moba_kernel,
        grid=(b, D_ATTN // LANES),
        in_specs=[full, full, full],
        out_specs=full,
        out_shape=jax.ShapeDtypeStruct((b, s, D_ATTN), BF16),
        scratch_shapes=[pltpu.VMEM((HEADS_PER_VREG, s, LANES), BF16),
                        pltpu.VMEM((HEADS_PER_VREG, PV_ROWS, s), BF16),
                        *[pltpu.VMEM((s, MOBA_BLOCK), F32)] * SCORE_BUFFERS],
        compiler_params=pltpu.CompilerParams(
            dimension_semantics=("arbitrary", "arbitrary"), vmem_limit_bytes=VMEM_LIMIT),
        name="moba",
    )(q, k, v)


def _merge_kernel(alpha, x_ref, ya_ref, pm_ref, g_ref, wa_ref, wp_ref, wo_ref, lg_ref, lb_ref, o_ref):
    for r in range(x_ref.shape[1] // LN_ROWS):
        rows = slice(r * LN_ROWS, (r + 1) * LN_ROWS)
        y_attn = _dot(ya_ref[0, rows, :], wa_ref[...])
        y_pool = _dot(pm_ref[0, rows, :], wp_ref[...])
        g_attn = g_ref[0, rows, 0:D_MODEL].astype(F32)
        g_pool = g_ref[0, rows, D_MODEL:2 * D_MODEL].astype(F32)
        mix = _dot((g_attn * y_attn + g_pool * y_pool).astype(BF16), wo_ref[...])
        o_ref[0, rows, :] = _layer_norm(alpha * x_ref[0, rows, :] + mix, lg_ref[...], lb_ref[...])


def _merge(alpha, x, y_attn, pm, gates, w_a, w_p, w_o, ln_g, ln_b):
    b, s, d = x.shape
    tm = ROW_TILE
    row = lambda n: pl.BlockSpec((1, tm, n), lambda bi, i: (bi, i, 0))
    return pl.pallas_call(
        functools.partial(_merge_kernel, alpha),
        grid=(b, s // tm),
        in_specs=[row(d), row(D_ATTN), row(D_POOL), row(2 * D_MODEL),
                  _const_spec(w_a.shape), _const_spec(w_p.shape), _const_spec(w_o.shape),
                  _const_spec((1, d)), _const_spec((1, d))],
        out_specs=row(d),
        out_shape=jax.ShapeDtypeStruct((b, s, d), F32),
        compiler_params=pltpu.CompilerParams(
            dimension_semantics=("arbitrary", "arbitrary"), vmem_limit_bytes=VMEM_LIMIT),
        name="merge",
    )(x, y_attn, pm, gates, w_a, w_p, w_o, ln_g, ln_b)


def _ffn_kernel(alpha, x_ref, wg_ref, wu_ref, cw_ref, cb_ref, wd_ref, lg_ref, lb_ref, o_ref,
                abuf, hbuf):
    i = pl.program_id(1)
    tm = x_ref.shape[1]
    halo = SUBLANES

    @pl.when(i == 0)
    def _():
        abuf[0:halo, :] = jnp.zeros((halo, D_FF), F32)

    x = x_ref[0]
    xb = x.astype(BF16)
    for c in range(D_FF // FF_CHUNK):
        sl = slice(c * FF_CHUNK, (c + 1) * FF_CHUNK)
        a = _dot(xb, wg_ref[:, sl])
        u = _dot(xb, wu_ref[:, sl])
        abuf[halo:halo + tm, sl] = a
        conv = a * cw_ref[CONV_WIDTH - 1:CONV_WIDTH, sl] + cb_ref[:, sl]
        for t in range(CONV_WIDTH - 1):
            back = CONV_WIDTH - 1 - t
            conv = conv + abuf[halo - back:halo - back + tm, sl] * cw_ref[t:t + 1, sl]
        abuf[0:halo, sl] = a[tm - halo:tm, :]
        gelu = 0.5 * conv * (1.0 + lax.erf(conv * (2.0 ** -0.5)))
        hbuf[:, sl] = (gelu * u).astype(BF16)
    for r in range(tm // LN_ROWS):
        rows = slice(r * LN_ROWS, (r + 1) * LN_ROWS)
        ffn = _dot(hbuf[rows, :], wd_ref[...])
        o_ref[0, rows, :] = _layer_norm(alpha * x_ref[0, rows, :] + ffn, lg_ref[...], lb_ref[...])


def _ffn(alpha, x, w_g, w_u, conv_w, conv_b, w_d, ln_g, ln_b):
    b, s, d = x.shape
    tm = ROW_TILE
    row = pl.BlockSpec((1, tm, d), lambda bi, i: (bi, i, 0))
    return pl.pallas_call(
        functools.partial(_ffn_kernel, alpha),
        grid=(b, s // tm),
        in_specs=[row, _const_spec(w_g.shape), _const_spec(w_u.shape),
                  _const_spec(conv_w.shape), _const_spec((1, D_FF)), _const_spec(w_d.shape),
                  _const_spec((1, d)), _const_spec((1, d))],
        out_specs=row,
        out_shape=jax.ShapeDtypeStruct((b, s, d), F32),
        scratch_shapes=[pltpu.VMEM((SUBLANES + tm, D_FF), F32),
                        pltpu.VMEM((tm, D_FF), BF16)],
        compiler_params=pltpu.CompilerParams(
            dimension_semantics=("arbitrary", "arbitrary"), vmem_limit_bytes=VMEM_LIMIT),
        name="convffn",
    )(x, w_g, w_u, conv_w, conv_b, w_d, ln_g, ln_b)


def _rope_tables(s):
    half = HEAD_DIM // 2
    inv_freq = 1.0 / (ROPE_THETA ** (jnp.arange(half, dtype=F32) / half))
    ang = jnp.arange(s, dtype=F32)[:, None] * inv_freq[None, :]
    cos, sin = jnp.cos(ang), jnp.sin(ang)
    cos = jnp.tile(jnp.concatenate([cos, cos], axis=-1), (1, HEADS_PER_VREG))
    sin = jnp.tile(jnp.concatenate([-sin, sin], axis=-1), (1, HEADS_PER_VREG))
    return cos, sin


def kernel(x, w_in, b_gate, w_branch_attn, w_pool, pool_scale, w_branch_pool, w_out, ln1_g, ln1_b,
           w_ffn_gate, w_ffn_up, conv_w, conv_b, w_ffn_down, ln2_g, ln2_b):
    depth = w_in.shape[0]
    s = x.shape[1]
    alpha = (2.0 * depth) ** 0.25
    cos, sin = _rope_tables(s)
    vec = lambda t: t.reshape(1, -1)
    for l in range(depth):
        q, k, v, pm, gates = _inproj(x, w_in[l].astype(BF16), vec(b_gate[l]), cos, sin,
                                     w_pool[l].astype(BF16), vec(pool_scale[l]))
        y_attn = _moba(q, k, v)
        x = _merge(alpha, x, y_attn, pm, gates, w_branch_attn[l].astype(BF16),
                   w_branch_pool[l].astype(BF16), w_out[l].astype(BF16), vec(ln1_g[l]), vec(ln1_b[l]))
        x = _ffn(alpha, x, w_ffn_gate[l].astype(BF16), w_ffn_up[l].astype(BF16), conv_w[l],
                 vec(conv_b[l]), w_ffn_down[l].astype(BF16), vec(ln2_g[l]), vec(ln2_b[l]))
    return x
```

```python
import functools

import jax
import jax.numpy as jnp
from jax import lax
from jax.experimental import pallas as pl
from jax.experimental.pallas import tpu as pltpu

D_MODEL = 1024
ATTN_HEADS = 8
HEAD_DIM = 64
D_ATTN = ATTN_HEADS * HEAD_DIM
MOBA_BLOCK = 256
MOBA_TOPK = 3
ROPE_THETA = 10000.0
POOL_WINDOWS = (2, 4, 8, 16)
POOL_GROUP_DIM = 128
D_POOL = len(POOL_WINDOWS) * POOL_GROUP_DIM
D_FF = 2816
CONV_WIDTH = 3
LN_EPS = 1e-5
NEG = -1e30
Q_SCALE = HEAD_DIM ** -0.5 * 1.4426950408889634

LANES = 128
SUBLANES = 8
HEADS_PER_VREG = LANES // HEAD_DIM
BF16_ROWS = 2 * SUBLANES
PV_ROWS = HEAD_DIM + BF16_ROWS
SCORE_BUFFERS = 3
SCORE_SPLIT = 2
POOL_HALO = 16
ROW_TILE = 512
FF_CHUNK = 256
LN_ROWS = 256
VMEM_LIMIT = 52 * 1024 * 1024

F32 = jnp.float32
BF16 = jnp.bfloat16


def _dot(a, b):
    return jnp.dot(a, b, preferred_element_type=F32)


def _const_spec(shape):
    return pl.BlockSpec(shape, lambda *_: (0,) * len(shape), pipeline_mode=pl.Buffered(1))


def _layer_norm(y, g, b):
    mu = jnp.mean(y, axis=-1, keepdims=True)
    d = y - mu
    var = jnp.mean(d * d, axis=-1, keepdims=True)
    return d * lax.rsqrt(var + LN_EPS) * g + b


def _inproj_kernel(x_ref, w_ref, bg_ref, cos_ref, sin_ref, wpool_ref, pscale_ref,
                   q_ref, k_ref, v_ref, pm_ref, g_ref, ubuf, *lvl):
    i = pl.program_id(1)
    tm = x_ref.shape[1]
    xb = x_ref[0].astype(BF16)

    lane = lax.broadcasted_iota(jnp.int32, (tm, LANES), 1)
    first_half = (lane % HEAD_DIM) < (HEAD_DIM // 2)
    cos = cos_ref[...]
    sin = sin_ref[...]

    def rope(t):
        partner = jnp.where(first_half, pltpu.roll(t, LANES - HEAD_DIM // 2, 1),
                            pltpu.roll(t, HEAD_DIM // 2, 1))
        return t * cos + partner * sin

    pad, cur, end = SUBLANES, SUBLANES + POOL_HALO, SUBLANES + POOL_HALO + tm

    @pl.when(i == 0)
    def _():
        ubuf[0:cur, :] = jnp.zeros((cur, D_POOL), F32)
        for level in lvl:
            level[0:pad, :] = jnp.zeros((pad, level.shape[1]), F32)

    o3 = 3 * D_ATTN
    ubuf[cur:end, :] = _dot(xb, w_ref[:, o3:o3 + D_POOL])

    zq = _dot(xb, w_ref[:, 0:D_ATTN])
    for c in range(D_ATTN // LANES):
        sl = slice(c * LANES, (c + 1) * LANES)
        q_ref[0, :, sl] = (rope(zq[:, sl]) * Q_SCALE).astype(BF16)
    zk = _dot(xb, w_ref[:, D_ATTN:2 * D_ATTN])
    for c in range(D_ATTN // LANES):
        sl = slice(c * LANES, (c + 1) * LANES)
        k_ref[0, :, sl] = rope(zk[:, sl]).astype(BF16)
    v_ref[0] = _dot(xb, w_ref[:, 2 * D_ATTN:3 * D_ATTN]).astype(BF16)

    o4 = o3 + D_POOL
    for c in range(2 * D_MODEL // D_ATTN):
        sl = slice(c * D_ATTN, (c + 1) * D_ATTN)
        zg = _dot(xb, w_ref[:, o4 + c * D_ATTN:o4 + (c + 1) * D_ATTN]) + bg_ref[:, sl]
        g_ref[0, :, sl] = jax.nn.sigmoid(zg).astype(BF16)

    pos = i * tm + lax.broadcasted_iota(jnp.int32, (tm, 1), 0)
    prev = ubuf
    for g, w in enumerate(POOL_WINDOWS):
        sl = slice(g * POOL_GROUP_DIM, (g + 1) * POOL_GROUP_DIM)
        sums = prev[pad:end, :] + prev[pad - w // 2:end - w // 2, :]
        if g + 1 < len(POOL_WINDOWS):
            lvl[g][pad:end, :] = sums[:, POOL_GROUP_DIM:]
            prev = lvl[g]
        inv_count = 1.0 / jnp.minimum(pos + 1, w).astype(F32)
        pooled = sums[POOL_HALO:, 0:POOL_GROUP_DIM] * inv_count - ubuf[cur:end, sl]
        mixed = _dot(pooled.astype(BF16), wpool_ref[g]) * pscale_ref[:, sl]
        pm_ref[0, :, sl] = mixed.astype(BF16)
    ubuf[pad:cur, :] = ubuf[end - POOL_HALO:end, :]


def _inproj(x, w_in, b_gate, cos, sin, w_pool, pool_scale):
    b, s, d = x.shape
    tm = ROW_TILE
    n_in = w_in.shape[1]
    assert all(w == 2 ** (g + 1) for g, w in enumerate(POOL_WINDOWS)) and POOL_WINDOWS[-1] <= POOL_HALO
    row = lambda n: pl.BlockSpec((1, tm, n), lambda bi, i: (bi, i, 0))
    return pl.pallas_call(
        _inproj_kernel,
        grid=(b, s // tm),
        in_specs=[
            row(d),
            _const_spec((d, n_in)),
            _const_spec((1, 2 * D_MODEL)),
            pl.BlockSpec((tm, LANES), lambda bi, i: (i, 0)),
            pl.BlockSpec((tm, LANES), lambda bi, i: (i, 0)),
            _const_spec(w_pool.shape),
            _const_spec((1, D_POOL)),
        ],
        out_specs=[row(D_ATTN), row(D_ATTN), row(D_ATTN), row(D_POOL), row(2 * D_MODEL)],
        out_shape=[
            jax.ShapeDtypeStruct((b, s, D_ATTN), BF16),
            jax.ShapeDtypeStruct((b, s, D_ATTN), BF16),
            jax.ShapeDtypeStruct((b, s, D_ATTN), BF16),
            jax.ShapeDtypeStruct((b, s, D_POOL), BF16),
            jax.ShapeDtypeStruct((b, s, 2 * D_MODEL), BF16),
        ],
        scratch_shapes=[pltpu.VMEM((SUBLANES + POOL_HALO + tm, D_POOL - g * POOL_GROUP_DIM), F32)
                        for g in range(len(POOL_WINDOWS))],
        compiler_params=pltpu.CompilerParams(
            dimension_semantics=("arbitrary", "arbitrary"), vmem_limit_bytes=VMEM_LIMIT),
        name="inproj",
    )(x, w_in, b_gate, cos, sin, w_pool, pool_scale)


def _moba_kernel(q_ref, k_ref, v_ref, o_ref, khat_ref, vt_ref, qt_ref, *s_refs):
    seq = k_ref.shape[1]
    nb = seq // MOBA_BLOCK
    blk = lambda j: slice(j * MOBA_BLOCK, (j + 1) * MOBA_BLOCK)
    lane = lax.broadcasted_iota(jnp.int32, (MOBA_BLOCK, LANES), 1)

    kf = k_ref[0].astype(F32).reshape(nb, MOBA_BLOCK, LANES)
    km = jnp.sum(kf, axis=1) * (1.0 / MOBA_BLOCK)
    hi = km.astype(BF16)
    kmb = jnp.concatenate([hi, (km - hi.astype(F32)).astype(BF16)], axis=0)

    for j in range(nb):
        kj = k_ref[0, blk(j), :]
        for h in range(HEADS_PER_VREG):
            onehot = jnp.where(lane == (1 - h) * HEAD_DIM + j, 1.0, 0.0).astype(BF16)
            khat_ref[h, blk(j), :] = jnp.where((lane // HEAD_DIM) == h, kj, onehot)
        vt = v_ref[0, blk(j), :].astype(F32).T.astype(BF16)
        for h in range(HEADS_PER_VREG):
            vt_ref[h, 0:HEAD_DIM, blk(j)] = vt[h * HEAD_DIM:(h + 1) * HEAD_DIM]
            vt_ref[h, HEAD_DIM:PV_ROWS, blk(j)] = jnp.ones((PV_ROWS - HEAD_DIM, MOBA_BLOCK), BF16)
        qt_ref[:, blk(j)] = q_ref[0, blk(j), :].astype(F32).T.astype(BF16)

    dim_i = lax.broadcasted_iota(jnp.int32, (LANES, MOBA_BLOCK), 0)
    key_i = lax.broadcasted_iota(jnp.int32, (MOBA_BLOCK, MOBA_BLOCK), 0)
    qry_i = lax.broadcasted_iota(jnp.int32, (MOBA_BLOCK, MOBA_BLOCK), 1)
    causal = key_i <= qry_i
    blk_i = lax.broadcasted_iota(jnp.int32, (nb, MOBA_BLOCK), 0)
    fold = lambda t: t.reshape(MOBA_BLOCK // SUBLANES, SUBLANES, MOBA_BLOCK)

    units = [(i, h) for i in range(nb) for h in range(HEADS_PER_VREG)]
    state = {}
    sbuf = lambda u: s_refs[units.index(u) % len(s_refs)]

    def prepare(u):
        i, h = u
        in_head = (dim_i // HEAD_DIM) == h
        q = qt_ref[:, blk(i)]
        if i > MOBA_TOPK:
            g2 = _dot(kmb, jnp.where(in_head, q, jnp.zeros_like(q)))
            gate = g2[0:nb] + g2[nb:2 * nb]
            beaten = jnp.zeros((nb, MOBA_BLOCK), jnp.int32)
            for m in range(i):
                gm = gate[m:m + 1, :]
                wins = (gm > gate) | ((gm == gate) & (m < blk_i))
                beaten = beaten + jnp.where(wins, 1, 0)
            bias = jnp.where((blk_i < i) & (beaten >= MOBA_TOPK), NEG, 0.0)
            base = (1 - h) * HEAD_DIM
            parts = [bias, jnp.zeros((LANES - base - nb, MOBA_BLOCK), F32)]
            if base:
                parts = [jnp.zeros((base, MOBA_BLOCK), F32)] + parts
            spare = jnp.concatenate(parts, axis=0).astype(BF16)
        else:
            spare = jnp.zeros_like(q)
        state[u] = dict(q_aug=jnp.where(in_head, q, spare),
                        m_run=jnp.full((SUBLANES, MOBA_BLOCK), NEG, F32),
                        acc=jnp.zeros((PV_ROWS, MOBA_BLOCK), F32))

    def score_matmul(u):
        q_aug = state[u].pop("q_aug")
        n = u[0] + 1
        for lo in range(0, n, SCORE_SPLIT):
            keys = slice(lo * MOBA_BLOCK, min(lo + SCORE_SPLIT, n) * MOBA_BLOCK)
            sbuf(u)[keys, :] = _dot(khat_ref[u[1], keys, :], q_aug)

    def masked_scores(u, j):
        s = sbuf(u)[blk(j), :]
        return jnp.where(causal, s, NEG) if j == u[0] else s

    def score_step(u, j):
        st = state[u]
        st["m_run"] = jnp.maximum(st["m_run"], jnp.max(fold(masked_scores(u, j)), axis=0))

    def end_scores(u):
        state[u]["m_fin"] = jnp.max(state[u]["m_run"], axis=0, keepdims=True)

    def value_step(u, j):
        st = state[u]
        p = jnp.exp2(masked_scores(u, j) - st["m_fin"])
        st["acc"] = st["acc"] + _dot(vt_ref[u[1], :, blk(j)], p.astype(BF16))

    def finish(u):
        i, h = u
        acc = state.pop(u)["acc"]
        out = acc[0:HEAD_DIM] / acc[HEAD_DIM:HEAD_DIM + 1]
        if h == 0:
            state["head0"] = out
        else:
            both = jnp.concatenate([state.pop("head0"), out], axis=0)
            o_ref[0, blk(i), :] = both.T.astype(BF16)

    def stage_tasks(t):
        unit = lambda d: units[t + d] if 0 <= t + d < len(units) else None
        stages = []
        if unit(3):
            stages.append([functools.partial(prepare, unit(3))])
        if unit(2):
            stages.append([functools.partial(score_matmul, unit(2))])
        if unit(1):
            stages.append([functools.partial(score_step, unit(1), j) for j in range(unit(1)[0] + 1)]
                          + [functools.partial(end_scores, unit(1))])
        if unit(0):
            stages.append([functools.partial(value_step, unit(0), j) for j in range(unit(0)[0] + 1)])
        return stages

    for t in range(-3, len(units)):
        stages = stage_tasks(t)
        n = max(len(tasks) for tasks in stages)
        for step in range(n):
            for tasks in stages:
                lo, hi_ = step * len(tasks) // n, (step + 1) * len(tasks) // n
                for task in tasks[lo:hi_]:
                    task()
        if t >= 0:
            finish(units[t])


def _moba(q, k, v):
    b, s, _ = q.shape
    full = pl.BlockSpec((1, s, LANES), lambda bi, p: (bi, 0, p))
    return pl.pallas_call(
        _moba_kernel,
        grid=(b, D_ATTN // LANES),
        in_specs=[full, full, full],
        out_specs=full,
        out_shape=jax.ShapeDtypeStruct((b, s, D_ATTN), BF16),
        scratch_shapes=[pltpu.VMEM((HEADS_PER_VREG, s, LANES), BF16),
                        pltpu.VMEM((HEADS_PER_VREG, PV_ROWS, s), BF16),
                        pltpu.VMEM((LANES, s), BF16),
                        *[pltpu.VMEM((s, MOBA_BLOCK), F32)] * SCORE_BUFFERS],
        compiler_params=pltpu.CompilerParams(
            dimension_semantics=("arbitrary", "arbitrary"), vmem_limit_bytes=VMEM_LIMIT),
        name="moba",
    )(q, k, v)


def _merge_kernel(alpha, x_ref, ya_ref, pm_ref, g_ref, wa_ref, wp_ref, wo_ref, lg_ref, lb_ref, o_ref):
    for r in range(x_ref.shape[1] // LN_ROWS):
        rows = slice(r * LN_ROWS, (r + 1) * LN_ROWS)
        y_attn = _dot(ya_ref[0, rows, :], wa_ref[...])
        y_pool = _dot(pm_ref[0, rows, :], wp_ref[...])
        g_attn = g_ref[0, rows, 0:D_MODEL].astype(F32)
        g_pool = g_ref[0, rows, D_MODEL:2 * D_MODEL].astype(F32)
        mix = _dot((g_attn * y_attn + g_pool * y_pool).astype(BF16), wo_ref[...])
        o_ref[0, rows, :] = _layer_norm(alpha * x_ref[0, rows, :] + mix, lg_ref[...], lb_ref[...])


def _merge(alpha, x, y_attn, pm, gates, w_a, w_p, w_o, ln_g, ln_b):
    b, s, d = x.shape
    tm = ROW_TILE
    row = lambda n: pl.BlockSpec((1, tm, n), lambda bi, i: (bi, i, 0))
    return pl.pallas_call(
        functools.partial(_merge_kernel, alpha),
        grid=(b, s // tm),
        in_specs=[row(d), row(D_ATTN), row(D_POOL), row(2 * D_MODEL),
                  _const_spec(w_a.shape), _const_spec(w_p.shape), _const_spec(w_o.shape),
                  _const_spec((1, d)), _const_spec((1, d))],
        out_specs=row(d),
        out_shape=jax.ShapeDtypeStruct((b, s, d), F32),
        compiler_params=pltpu.CompilerParams(
            dimension_semantics=("arbitrary", "arbitrary"), vmem_limit_bytes=VMEM_LIMIT),
        name="merge",
    )(x, y_attn, pm, gates, w_a, w_p, w_o, ln_g, ln_b)


def _ffn_kernel(alpha, x_ref, wg_ref, wu_ref, cw_ref, cb_ref, wd_ref, lg_ref, lb_ref, o_ref,
                abuf, hbuf):
    i = pl.program_id(1)
    tm = x_ref.shape[1]
    halo = SUBLANES

    @pl.when(i == 0)
    def _():
        abuf[0:halo, :] = jnp.zeros((halo, D_FF), F32)

    x = x_ref[0]
    xb = x.astype(BF16)
    for c in range(D_FF // FF_CHUNK):
        sl = slice(c * FF_CHUNK, (c + 1) * FF_CHUNK)
        a = _dot(xb, wg_ref[:, sl])
        u = _dot(xb, wu_ref[:, sl])
        abuf[halo:halo + tm, sl] = a
        conv = a * cw_ref[CONV_WIDTH - 1:CONV_WIDTH, sl] + cb_ref[:, sl]
        for t in range(CONV_WIDTH - 1):
            back = CONV_WIDTH - 1 - t
            conv = conv + abuf[halo - back:halo - back + tm, sl] * cw_ref[t:t + 1, sl]
        abuf[0:halo, sl] = a[tm - halo:tm, :]
        gelu = 0.5 * conv * (1.0 + lax.erf(conv * (2.0 ** -0.5)))
        hbuf[:, sl] = (gelu * u).astype(BF16)
    for r in range(tm // LN_ROWS):
        rows = slice(r * LN_ROWS, (r + 1) * LN_ROWS)
        ffn = _dot(hbuf[rows, :], wd_ref[...])
        o_ref[0, rows, :] = _layer_norm(alpha * x_ref[0, rows, :] + ffn, lg_ref[...], lb_ref[...])


def _ffn(alpha, x, w_g, w_u, conv_w, conv_b, w_d, ln_g, ln_b):
    b, s, d = x.shape
    tm = ROW_TILE
    row = pl.BlockSpec((1, tm, d), lambda bi, i: (bi, i, 0))
    return pl.pallas_call(
        functools.partial(_ffn_kernel, alpha),
        grid=(b, s // tm),
        in_specs=[row, _const_spec(w_g.shape), _const_spec(w_u.shape),
                  _const_spec(conv_w.shape), _const_spec((1, D_FF)), _const_spec(w_d.shape),
                  _const_spec((1, d)), _const_spec((1, d))],
        out_specs=row,
        out_shape=jax.ShapeDtypeStruct((b, s, d), F32),
        scratch_shapes=[pltpu.VMEM((SUBLANES + tm, D_FF), F32),
                        pltpu.VMEM((tm, D_FF), BF16)],
        compiler_params=pltpu.CompilerParams(
            dimension_semantics=("arbitrary", "arbitrary"), vmem_limit_bytes=VMEM_LIMIT),
        name="convffn",
    )(x, w_g, w_u, conv_w, conv_b, w_d, ln_g, ln_b)


def _rope_tables(s):
    half = HEAD_DIM // 2
    inv_freq = 1.0 / (ROPE_THETA ** (jnp.arange(half, dtype=F32) / half))
    ang = jnp.arange(s, dtype=F32)[:, None] * inv_freq[None, :]
    cos, sin = jnp.cos(ang), jnp.sin(ang)
    cos = jnp.tile(jnp.concatenate([cos, cos], axis=-1), (1, HEADS_PER_VREG))
    sin = jnp.tile(jnp.concatenate([-sin, sin], axis=-1), (1, HEADS_PER_VREG))
    return cos, sin


def kernel(x, w_in, b_gate, w_branch_attn, w_pool, pool_scale, w_branch_pool, w_out, ln1_g, ln1_b,
           w_ffn_gate, w_ffn_up, conv_w, conv_b, w_ffn_down, ln2_g, ln2_b):
    depth = w_in.shape[0]
    s = x.shape[1]
    alpha = (2.0 * depth) ** 0.25
    cos, sin = _rope_tables(s)
    vec = lambda t: t.reshape(1, -1)
    for l in range(depth):
        q, k, v, pm, gates = _inproj(x, w_in[l].astype(BF16), vec(b_gate[l]), cos, sin,
                                     w_pool[l].astype(BF16), vec(pool_scale[l]))
        y_attn = _moba(q, k, v)
        x = _merge(alpha, x, y_attn, pm, gates, w_branch_attn[l].astype(BF16),
                   w_branch_pool[l].astype(BF16), w_out[l].astype(BF16), vec(ln1_g[l]), vec(ln1_b[l]))
        x = _ffn(alpha, x, w_ffn_gate[l].astype(BF16), w_ffn_up[l].astype(BF16), conv_w[l],
                 vec(conv_b[l]), w_ffn_down[l].astype(BF16), vec(ln2_g[l]), vec(ln2_b[l]))
    return x
```

```python
import functools

import jax
import jax.numpy as jnp
from jax import lax
from jax.experimental import pallas as pl
from jax.experimental.pallas import tpu as pltpu

D_MODEL = 1024
ATTN_HEADS = 8
HEAD_DIM = 64
D_ATTN = ATTN_HEADS * HEAD_DIM
MOBA_BLOCK = 256
MOBA_TOPK = 3
ROPE_THETA = 10000.0
POOL_WINDOWS = (2, 4, 8, 16)
POOL_GROUP_DIM = 128
D_POOL = len(POOL_WINDOWS) * POOL_GROUP_DIM
D_FF = 2816
CONV_WIDTH = 3
LN_EPS = 1e-5
NEG = -1e30
Q_SCALE = HEAD_DIM ** -0.5 * 1.4426950408889634

LANES = 128
SUBLANES = 8
HEADS_PER_VREG = LANES // HEAD_DIM
BF16_ROWS = 2 * SUBLANES
PV_ROWS = HEAD_DIM + BF16_ROWS
SCORE_BUFFERS = 3
SCORE_SPLIT = 2
POOL_HALO = 16
ROW_TILE = 512
FF_CHUNK = 256
LN_ROWS = 256
VMEM_LIMIT = 52 * 1024 * 1024

F32 = jnp.float32
BF16 = jnp.bfloat16


def _dot(a, b):
    return jnp.dot(a, b, preferred_element_type=F32)


def _const_spec(shape):
    return pl.BlockSpec(shape, lambda *_: (0,) * len(shape), pipeline_mode=pl.Buffered(1))


def _layer_norm(y, g, b):
    mu = jnp.mean(y, axis=-1, keepdims=True)
    d = y - mu
    var = jnp.mean(d * d, axis=-1, keepdims=True)
    return d * lax.rsqrt(var + LN_EPS) * g + b


def _inproj_kernel(x_ref, w_ref, bg_ref, cos_ref, sin_ref, wpool_ref, pscale_ref,
                   q_ref, k_ref, v_ref, pm_ref, g_ref, ubuf, *lvl):
    i = pl.program_id(1)
    tm = x_ref.shape[1]
    xb = x_ref[0].astype(BF16)

    lane = lax.broadcasted_iota(jnp.int32, (tm, LANES), 1)
    first_half = (lane % HEAD_DIM) < (HEAD_DIM // 2)
    cos = cos_ref[...]
    sin = sin_ref[...]

    def rope(t):
        partner = jnp.where(first_half, pltpu.roll(t, LANES - HEAD_DIM // 2, 1),
                            pltpu.roll(t, HEAD_DIM // 2, 1))
        return t * cos + partner * sin

    pad, cur, end = SUBLANES, SUBLANES + POOL_HALO, SUBLANES + POOL_HALO + tm

    @pl.when(i == 0)
    def _():
        ubuf[0:cur, :] = jnp.zeros((cur, D_POOL), F32)
        for level in lvl:
            level[0:pad, :] = jnp.zeros((pad, level.shape[1]), F32)

    o3 = 3 * D_ATTN
    ubuf[cur:end, :] = _dot(xb, w_ref[:, o3:o3 + D_POOL])

    zq = _dot(xb, w_ref[:, 0:D_ATTN])
    for c in range(D_ATTN // LANES):
        sl = slice(c * LANES, (c + 1) * LANES)
        q_ref[0, :, sl] = (rope(zq[:, sl]) * Q_SCALE).astype(BF16)
    zk = _dot(xb, w_ref[:, D_ATTN:2 * D_ATTN])
    for c in range(D_ATTN // LANES):
        sl = slice(c * LANES, (c + 1) * LANES)
        k_ref[0, :, sl] = rope(zk[:, sl]).astype(BF16)
    v_ref[0] = _dot(xb, w_ref[:, 2 * D_ATTN:3 * D_ATTN]).astype(BF16)

    o4 = o3 + D_POOL
    for c in range(2 * D_MODEL // D_ATTN):
        sl = slice(c * D_ATTN, (c + 1) * D_ATTN)
        zg = _dot(xb, w_ref[:, o4 + c * D_ATTN:o4 + (c + 1) * D_ATTN]) + bg_ref[:, sl]
        g_ref[0, :, sl] = jax.nn.sigmoid(zg).astype(BF16)

    pos = i * tm + lax.broadcasted_iota(jnp.int32, (tm, 1), 0)
    prev = ubuf
    for g, w in enumerate(POOL_WINDOWS):
        sl = slice(g * POOL_GROUP_DIM, (g + 1) * POOL_GROUP_DIM)
        sums = prev[pad:end, :] + prev[pad - w // 2:end - w // 2, :]
        if g + 1 < len(POOL_WINDOWS):
            lvl[g][pad:end, :] = sums[:, POOL_GROUP_DIM:]
            prev = lvl[g]
        inv_count = 1.0 / jnp.minimum(pos + 1, w).astype(F32)
        pooled = sums[POOL_HALO:, 0:POOL_GROUP_DIM] * inv_count - ubuf[cur:end, sl]
        mixed = _dot(pooled.astype(BF16), wpool_ref[g]) * pscale_ref[:, sl]
        pm_ref[0, :, sl] = mixed.astype(BF16)
    ubuf[pad:cur, :] = ubuf[end - POOL_HALO:end, :]


def _inproj(x, w_in, b_gate, cos, sin, w_pool, pool_scale):
    b, s, d = x.shape
    tm = ROW_TILE
    n_in = w_in.shape[1]
    assert all(w == 2 ** (g + 1) for g, w in enumerate(POOL_WINDOWS)) and POOL_WINDOWS[-1] <= POOL_HALO
    row = lambda n: pl.BlockSpec((1, tm, n), lambda bi, i: (bi, i, 0))
    return pl.pallas_call(
        _inproj_kernel,
        grid=(b, s // tm),
        in_specs=[
            row(d),
            _const_spec((d, n_in)),
            _const_spec((1, 2 * D_MODEL)),
            pl.BlockSpec((tm, LANES), lambda bi, i: (i, 0)),
            pl.BlockSpec((tm, LANES), lambda bi, i: (i, 0)),
            _const_spec(w_pool.shape),
            _const_spec((1, D_POOL)),
        ],
        out_specs=[row(D_ATTN), row(D_ATTN), row(D_ATTN), row(D_POOL), row(2 * D_MODEL)],
        out_shape=[
            jax.ShapeDtypeStruct((b, s, D_ATTN), BF16),
            jax.ShapeDtypeStruct((b, s, D_ATTN), BF16),
            jax.ShapeDtypeStruct((b, s, D_ATTN), BF16),
            jax.ShapeDtypeStruct((b, s, D_POOL), BF16),
            jax.ShapeDtypeStruct((b, s, 2 * D_MODEL), BF16),
        ],
        scratch_shapes=[pltpu.VMEM((SUBLANES + POOL_HALO + tm, D_POOL - g * POOL_GROUP_DIM), F32)
                        for g in range(len(POOL_WINDOWS))],
        compiler_params=pltpu.CompilerParams(
            dimension_semantics=("arbitrary", "arbitrary"), vmem_limit_bytes=VMEM_LIMIT),
        name="inproj",
    )(x, w_in, b_gate, cos, sin, w_pool, pool_scale)


def _moba_kernel(n_cast, q_ref, k_ref, v_ref, *refs):
    w_refs, o_ref, wb_refs = refs[:n_cast], refs[n_cast], refs[n_cast + 1:2 * n_cast + 1]
    khat_ref, vt_ref, qt_ref, *s_refs = refs[2 * n_cast + 1:]
    for w_ref, wb_ref in zip(w_refs, wb_refs):
        wb_ref[...] = w_ref[0].astype(BF16)

    seq = k_ref.shape[1]
    nb = seq // MOBA_BLOCK
    blk = lambda j: slice(j * MOBA_BLOCK, (j + 1) * MOBA_BLOCK)
    lane = lax.broadcasted_iota(jnp.int32, (MOBA_BLOCK, LANES), 1)

    kf = k_ref[0].astype(F32).reshape(nb, MOBA_BLOCK, LANES)
    km = jnp.sum(kf, axis=1) * (1.0 / MOBA_BLOCK)
    hi = km.astype(BF16)
    kmb = jnp.concatenate([hi, (km - hi.astype(F32)).astype(BF16)], axis=0)

    for j in range(nb):
        kj = k_ref[0, blk(j), :]
        for h in range(HEADS_PER_VREG):
            onehot = jnp.where(lane == (1 - h) * HEAD_DIM + j, 1.0, 0.0).astype(BF16)
            khat_ref[h, blk(j), :] = jnp.where((lane // HEAD_DIM) == h, kj, onehot)
        vt = v_ref[0, blk(j), :].astype(F32).T.astype(BF16)
        for h in range(HEADS_PER_VREG):
            vt_ref[h, 0:HEAD_DIM, blk(j)] = vt[h * HEAD_DIM:(h + 1) * HEAD_DIM]
            vt_ref[h, HEAD_DIM:PV_ROWS, blk(j)] = jnp.ones((PV_ROWS - HEAD_DIM, MOBA_BLOCK), BF16)
        qt_ref[:, blk(j)] = q_ref[0, blk(j), :].astype(F32).T.astype(BF16)

    dim_i = lax.broadcasted_iota(jnp.int32, (LANES, MOBA_BLOCK), 0)
    key_i = lax.broadcasted_iota(jnp.int32, (MOBA_BLOCK, MOBA_BLOCK), 0)
    qry_i = lax.broadcasted_iota(jnp.int32, (MOBA_BLOCK, MOBA_BLOCK), 1)
    causal = key_i <= qry_i
    blk_i = lax.broadcasted_iota(jnp.int32, (nb, MOBA_BLOCK), 0)
    fold = lambda t: t.reshape(MOBA_BLOCK // SUBLANES, SUBLANES, MOBA_BLOCK)

    units = [(i, h) for i in range(nb) for h in range(HEADS_PER_VREG)]
    state = {}
    sbuf = lambda u: s_refs[units.index(u) % len(s_refs)]

    def prepare(u):
        i, h = u
        in_head = (dim_i // HEAD_DIM) == h
        q = qt_ref[:, blk(i)]
        if i > MOBA_TOPK:
            g2 = _dot(kmb, jnp.where(in_head, q, jnp.zeros_like(q)))
            gate = g2[0:nb] + g2[nb:2 * nb]
            beaten = jnp.zeros((nb, MOBA_BLOCK), jnp.int32)
            for m in range(i):
                gm = gate[m:m + 1, :]
                wins = (gm > gate) | ((gm == gate) & (m < blk_i))
                beaten = beaten + jnp.where(wins, 1, 0)
            bias = jnp.where((blk_i < i) & (beaten >= MOBA_TOPK), NEG, 0.0)
            base = (1 - h) * HEAD_DIM
            parts = [bias, jnp.zeros((LANES - base - nb, MOBA_BLOCK), F32)]
            if base:
                parts = [jnp.zeros((base, MOBA_BLOCK), F32)] + parts
            spare = jnp.concatenate(parts, axis=0).astype(BF16)
        else:
            spare = jnp.zeros_like(q)
        state[u] = dict(q_aug=jnp.where(in_head, q, spare),
                        m_run=jnp.full((SUBLANES, MOBA_BLOCK), NEG, F32),
                        acc=jnp.zeros((PV_ROWS, MOBA_BLOCK), F32))

    def score_matmul(u):
        q_aug = state[u].pop("q_aug")
        n = u[0] + 1
        for lo in range(0, n, SCORE_SPLIT):
            keys = slice(lo * MOBA_BLOCK, min(lo + SCORE_SPLIT, n) * MOBA_BLOCK)
            sbuf(u)[keys, :] = _dot(khat_ref[u[1], keys, :], q_aug)

    def masked_scores(u, j):
        s = sbuf(u)[blk(j), :]
        return jnp.where(causal, s, NEG) if j == u[0] else s

    def score_step(u, j):
        st = state[u]
        st["m_run"] = jnp.maximum(st["m_run"], jnp.max(fold(masked_scores(u, j)), axis=0))

    def end_scores(u):
        state[u]["m_fin"] = jnp.max(state[u]["m_run"], axis=0, keepdims=True)

    def value_step(u, j):
        st = state[u]
        p = jnp.exp2(masked_scores(u, j) - st["m_fin"])
        st["acc"] = st["acc"] + _dot(vt_ref[u[1], :, blk(j)], p.astype(BF16))

    def finish(u):
        i, h = u
        acc = state.pop(u)["acc"]
        out = acc[0:HEAD_DIM] / acc[HEAD_DIM:HEAD_DIM + 1]
        if h == 0:
            state["head0"] = out
        else:
            both = jnp.concatenate([state.pop("head0"), out], axis=0)
            o_ref[0, blk(i), :] = both.T.astype(BF16)

    def stage_tasks(t):
        unit = lambda d: units[t + d] if 0 <= t + d < len(units) else None
        stages = []
        if unit(3):
            stages.append([functools.partial(prepare, unit(3))])
        if unit(2):
            stages.append([functools.partial(score_matmul, unit(2))])
        if unit(1):
            stages.append([functools.partial(score_step, unit(1), j) for j in range(unit(1)[0] + 1)]
                          + [functools.partial(end_scores, unit(1))])
        if unit(0):
            stages.append([functools.partial(value_step, unit(0), j) for j in range(unit(0)[0] + 1)])
        return stages

    for t in range(-3, len(units)):
        stages = stage_tasks(t)
        n = max(len(tasks) for tasks in stages)
        for step in range(n):
            for tasks in stages:
                lo, hi_ = step * len(tasks) // n, (step + 1) * len(tasks) // n
                for task in tasks[lo:hi_]:
                    task()
        if t >= 0:
            finish(units[t])


def _cast_rows(rows, steps):
    r = -(-rows // steps)
    r = -(-r // BF16_ROWS) * BF16_ROWS
    while rows % r:
        r += BF16_ROWS
    return r


def _moba(q, k, v, layer, weights):
    b, s, _ = q.shape
    n_pairs = D_ATTN // LANES
    full = pl.BlockSpec((1, s, LANES), lambda bi, p: (bi, 0, p))
    w_in_specs, w_out_specs, w_out_shapes = [], [], []
    for w in weights:
        _, rows, cols = w.shape
        r = _cast_rows(rows, b * n_pairs)
        last = rows // r - 1
        w_in_specs.append(pl.BlockSpec(
            (1, r, cols), lambda bi, p, last=last: (layer, jnp.minimum(bi * n_pairs + p, last), 0)))
        w_out_specs.append(pl.BlockSpec(
            (r, cols), lambda bi, p, last=last: (jnp.minimum(bi * n_pairs + p, last), 0)))
        w_out_shapes.append(jax.ShapeDtypeStruct((rows, cols), BF16))
    y_attn, *w_bf16 = pl.pallas_call(
        functools.partial(_moba_kernel, len(weights)),
        grid=(b, n_pairs),
        in_specs=[full, full, full] + w_in_specs,
        out_specs=[full] + w_out_specs,
        out_shape=[jax.ShapeDtypeStruct((b, s, D_ATTN), BF16)] + w_out_shapes,
        scratch_shapes=[pltpu.VMEM((HEADS_PER_VREG, s, LANES), BF16),
                        pltpu.VMEM((HEADS_PER_VREG, PV_ROWS, s), BF16),
                        pltpu.VMEM((LANES, s), BF16),
                        *[pltpu.VMEM((s, MOBA_BLOCK), F32)] * SCORE_BUFFERS],
        compiler_params=pltpu.CompilerParams(
            dimension_semantics=("arbitrary", "arbitrary"), vmem_limit_bytes=VMEM_LIMIT),
        name="moba",
    )(q, k, v, *weights)
    return y_attn, w_bf16


def _merge_kernel(alpha, x_ref, ya_ref, pm_ref, g_ref, wa_ref, wp_ref, wo_ref, lg_ref, lb_ref, o_ref):
    for r in range(x_ref.shape[1] // LN_ROWS):
        rows = slice(r * LN_ROWS, (r + 1) * LN_ROWS)
        y_attn = _dot(ya_ref[0, rows, :], wa_ref[...])
        y_pool = _dot(pm_ref[0, rows, :], wp_ref[...])
        g_attn = g_ref[0, rows, 0:D_MODEL].astype(F32)
        g_pool = g_ref[0, rows, D_MODEL:2 * D_MODEL].astype(F32)
        mix = _dot((g_attn * y_attn + g_pool * y_pool).astype(BF16), wo_ref[...])
        o_ref[0, rows, :] = _layer_norm(alpha * x_ref[0, rows, :] + mix, lg_ref[...], lb_ref[...])


def _merge(alpha, x, y_attn, pm, gates, w_a, w_p, w_o, ln_g, ln_b):
    b, s, d = x.shape
    tm = ROW_TILE
    row = lambda n: pl.BlockSpec((1, tm, n), lambda bi, i: (bi, i, 0))
    return pl.pallas_call(
        functools.partial(_merge_kernel, alpha),
        grid=(b, s // tm),
        in_specs=[row(d), row(D_ATTN), row(D_POOL), row(2 * D_MODEL),
                  _const_spec(w_a.shape), _const_spec(w_p.shape), _const_spec(w_o.shape),
                  _const_spec((1, d)), _const_spec((1, d))],
        out_specs=row(d),
        out_shape=jax.ShapeDtypeStruct((b, s, d), F32),
        compiler_params=pltpu.CompilerParams(
            dimension_semantics=("arbitrary", "arbitrary"), vmem_limit_bytes=VMEM_LIMIT),
        name="merge",
    )(x, y_attn, pm, gates, w_a, w_p, w_o, ln_g, ln_b)


def _ffn_kernel(alpha, x_ref, wg_ref, wu_ref, cw_ref, cb_ref, wd_ref, lg_ref, lb_ref, o_ref,
                abuf, hbuf):
    i = pl.program_id(1)
    tm = x_ref.shape[1]
    halo = SUBLANES

    @pl.when(i == 0)
    def _():
        abuf[0:halo, :] = jnp.zeros((halo, D_FF), F32)

    x = x_ref[0]
    xb = x.astype(BF16)
    for c in range(D_FF // FF_CHUNK):
        sl = slice(c * FF_CHUNK, (c + 1) * FF_CHUNK)
        a = _dot(xb, wg_ref[:, sl])
        u = _dot(xb, wu_ref[:, sl])
        abuf[halo:halo + tm, sl] = a
        conv = a * cw_ref[CONV_WIDTH - 1:CONV_WIDTH, sl] + cb_ref[:, sl]
        for t in range(CONV_WIDTH - 1):
            back = CONV_WIDTH - 1 - t
            conv = conv + abuf[halo - back:halo - back + tm, sl] * cw_ref[t:t + 1, sl]
        abuf[0:halo, sl] = a[tm - halo:tm, :]
        gelu = 0.5 * conv * (1.0 + lax.erf(conv * (2.0 ** -0.5)))
        hbuf[:, sl] = (gelu * u).astype(BF16)
    for r in range(tm // LN_ROWS):
        rows = slice(r * LN_ROWS, (r + 1) * LN_ROWS)
        ffn = _dot(hbuf[rows, :], wd_ref[...])
        o_ref[0, rows, :] = _layer_norm(alpha * x_ref[0, rows, :] + ffn, lg_ref[...], lb_ref[...])


def _ffn(alpha, x, w_g, w_u, conv_w, conv_b, w_d, ln_g, ln_b):
    b, s, d = x.shape
    tm = ROW_TILE
    row = pl.BlockSpec((1, tm, d), lambda bi, i: (bi, i, 0))
    return pl.pallas_call(
        functools.partial(_ffn_kernel, alpha),
        grid=(b, s // tm),
        in_specs=[row, _const_spec(w_g.shape), _const_spec(w_u.shape),
                  _const_spec(conv_w.shape), _const_spec((1, D_FF)), _const_spec(w_d.shape),
                  _const_spec((1, d)), _const_spec((1, d))],
        out_specs=row,
        out_shape=jax.ShapeDtypeStruct((b, s, d), F32),
        scratch_shapes=[pltpu.VMEM((SUBLANES + tm, D_FF), F32),
                        pltpu.VMEM((tm, D_FF), BF16)],
        compiler_params=pltpu.CompilerParams(
            dimension_semantics=("arbitrary", "arbitrary"), vmem_limit_bytes=VMEM_LIMIT),
        name="convffn",
    )(x, w_g, w_u, conv_w, conv_b, w_d, ln_g, ln_b)


def _rope_tables(s):
    half = HEAD_DIM // 2
    inv_freq = 1.0 / (ROPE_THETA ** (jnp.arange(half, dtype=F32) / half))
    ang = jnp.arange(s, dtype=F32)[:, None] * inv_freq[None, :]
    cos, sin = jnp.cos(ang), jnp.sin(ang)
    cos = jnp.tile(jnp.concatenate([cos, cos], axis=-1), (1, HEADS_PER_VREG))
    sin = jnp.tile(jnp.concatenate([-sin, sin], axis=-1), (1, HEADS_PER_VREG))
    return cos, sin


def kernel(x, w_in, b_gate, w_branch_attn, w_pool, pool_scale, w_branch_pool, w_out, ln1_g, ln1_b,
           w_ffn_gate, w_ffn_up, conv_w, conv_b, w_ffn_down, ln2_g, ln2_b):
    depth = w_in.shape[0]
    s = x.shape[1]
    alpha = (2.0 * depth) ** 0.25
    cos, sin = _rope_tables(s)
    vec = lambda t: t.reshape(1, -1)
    for l in range(depth):
        q, k, v, pm, gates = _inproj(x, w_in[l].astype(BF16), vec(b_gate[l]), cos, sin,
                                     w_pool[l].astype(BF16), vec(pool_scale[l]))
        y_attn, (w_a, w_p, w_o, w_g, w_u, w_d) = _moba(
            q, k, v, l, [w_branch_attn, w_branch_pool, w_out, w_ffn_gate, w_ffn_up, w_ffn_down])
        x = _merge(alpha, x, y_attn, pm, gates, w_a, w_p, w_o, vec(ln1_g[l]), vec(ln1_b[l]))
        x = _ffn(alpha, x, w_g, w_u, conv_w[l], vec(conv_b[l]), w_d, vec(ln2_g[l]), vec(ln2_b[l]))
    return x
```

```python
import functools

import jax
import jax.numpy as jnp
from jax import lax
from jax.experimental import pallas as pl
from jax.experimental.pallas import tpu as pltpu

D_MODEL = 1024
ATTN_HEADS = 8
HEAD_DIM = 64
D_ATTN = ATTN_HEADS * HEAD_DIM
MOBA_BLOCK = 256
MOBA_TOPK = 3
ROPE_THETA = 10000.0
POOL_WINDOWS = (2, 4, 8, 16)
POOL_GROUP_DIM = 128
D_POOL = len(POOL_WINDOWS) * POOL_GROUP_DIM
D_FF = 2816
CONV_WIDTH = 3
LN_EPS = 1e-5
NEG = -1e30
Q_SCALE = HEAD_DIM ** -0.5 * 1.4426950408889634

LANES = 128
SUBLANES = 8
HEADS_PER_VREG = LANES // HEAD_DIM
BF16_ROWS = 2 * SUBLANES
PV_ROWS = HEAD_DIM + BF16_ROWS
SCORE_BUFFERS = 3
SCORE_SPLIT = 2
POOL_HALO = 16
ROW_TILE = 512
FF_CHUNK = 256
LN_ROWS = 256
VMEM_LIMIT = 52 * 1024 * 1024

F32 = jnp.float32
BF16 = jnp.bfloat16


def _dot(a, b):
    return jnp.dot(a, b, preferred_element_type=F32)


def _const_spec(shape):
    return pl.BlockSpec(shape, lambda *_: (0,) * len(shape), pipeline_mode=pl.Buffered(1))


def _layer_norm(y, g, b):
    mu = jnp.mean(y, axis=-1, keepdims=True)
    d = y - mu
    var = jnp.mean(d * d, axis=-1, keepdims=True)
    return d * lax.rsqrt(var + LN_EPS) * g + b


def _inproj_kernel(x_ref, w_ref, bg_ref, cos_ref, sin_ref, wpool_ref, pscale_ref,
                   q_ref, k_ref, v_ref, pm_ref, g_ref, ubuf, *lvl):
    i = pl.program_id(1)
    tm = x_ref.shape[1]
    xb = x_ref[0].astype(BF16)

    lane = lax.broadcasted_iota(jnp.int32, (tm, LANES), 1)
    first_half = (lane % HEAD_DIM) < (HEAD_DIM // 2)
    cos = cos_ref[...]
    sin = sin_ref[...]

    def rope(t):
        partner = jnp.where(first_half, pltpu.roll(t, LANES - HEAD_DIM // 2, 1),
                            pltpu.roll(t, HEAD_DIM // 2, 1))
        return t * cos + partner * sin

    pad, cur, end = SUBLANES, SUBLANES + POOL_HALO, SUBLANES + POOL_HALO + tm

    @pl.when(i == 0)
    def _():
        ubuf[0:cur, :] = jnp.zeros((cur, D_POOL), F32)
        for level in lvl:
            level[0:pad, :] = jnp.zeros((pad, level.shape[1]), F32)

    o3 = 3 * D_ATTN
    ubuf[cur:end, :] = _dot(xb, w_ref[:, o3:o3 + D_POOL])

    zq = _dot(xb, w_ref[:, 0:D_ATTN])
    for c in range(D_ATTN // LANES):
        sl = slice(c * LANES, (c + 1) * LANES)
        q_ref[0, :, sl] = (rope(zq[:, sl]) * Q_SCALE).astype(BF16)
    zk = _dot(xb, w_ref[:, D_ATTN:2 * D_ATTN])
    for c in range(D_ATTN // LANES):
        sl = slice(c * LANES, (c + 1) * LANES)
        k_ref[0, :, sl] = rope(zk[:, sl]).astype(BF16)
    v_ref[0] = _dot(xb, w_ref[:, 2 * D_ATTN:3 * D_ATTN]).astype(BF16)

    o4 = o3 + D_POOL
    for c in range(2 * D_MODEL // D_ATTN):
        sl = slice(c * D_ATTN, (c + 1) * D_ATTN)
        zg = _dot(xb, w_ref[:, o4 + c * D_ATTN:o4 + (c + 1) * D_ATTN]) + bg_ref[:, sl]
        g_ref[0, :, sl] = jax.nn.sigmoid(zg).astype(BF16)

    pos = i * tm + lax.broadcasted_iota(jnp.int32, (tm, 1), 0)
    prev = ubuf
    for g, w in enumerate(POOL_WINDOWS):
        sl = slice(g * POOL_GROUP_DIM, (g + 1) * POOL_GROUP_DIM)
        sums = prev[pad:end, :] + prev[pad - w // 2:end - w // 2, :]
        if g + 1 < len(POOL_WINDOWS):
            lvl[g][pad:end, :] = sums[:, POOL_GROUP_DIM:]
            prev = lvl[g]
        inv_count = 1.0 / jnp.minimum(pos + 1, w).astype(F32)
        pooled = sums[POOL_HALO:, 0:POOL_GROUP_DIM] * inv_count - ubuf[cur:end, sl]
        mixed = _dot(pooled.astype(BF16), wpool_ref[g]) * pscale_ref[:, sl]
        pm_ref[0, :, sl] = mixed.astype(BF16)
    ubuf[pad:cur, :] = ubuf[end - POOL_HALO:end, :]


def _inproj(x, w_in, b_gate, cos, sin, w_pool, pool_scale):
    b, s, d = x.shape
    tm = ROW_TILE
    n_in = w_in.shape[1]
    assert all(w == 2 ** (g + 1) for g, w in enumerate(POOL_WINDOWS)) and POOL_WINDOWS[-1] <= POOL_HALO
    row = lambda n: pl.BlockSpec((1, tm, n), lambda bi, i: (bi, i, 0))
    return pl.pallas_call(
        _inproj_kernel,
        grid=(b, s // tm),
        in_specs=[
            row(d),
            _const_spec((d, n_in)),
            _const_spec((1, 2 * D_MODEL)),
            pl.BlockSpec((tm, LANES), lambda bi, i: (i, 0)),
            pl.BlockSpec((tm, LANES), lambda bi, i: (i, 0)),
            _const_spec(w_pool.shape),
            _const_spec((1, D_POOL)),
        ],
        out_specs=[row(D_ATTN), row(D_ATTN), row(D_ATTN), row(D_POOL), row(2 * D_MODEL)],
        out_shape=[
            jax.ShapeDtypeStruct((b, s, D_ATTN), BF16),
            jax.ShapeDtypeStruct((b, s, D_ATTN), BF16),
            jax.ShapeDtypeStruct((b, s, D_ATTN), BF16),
            jax.ShapeDtypeStruct((b, s, D_POOL), BF16),
            jax.ShapeDtypeStruct((b, s, 2 * D_MODEL), BF16),
        ],
        scratch_shapes=[pltpu.VMEM((SUBLANES + POOL_HALO + tm, D_POOL - g * POOL_GROUP_DIM), F32)
                        for g in range(len(POOL_WINDOWS))],
        compiler_params=pltpu.CompilerParams(
            dimension_semantics=("arbitrary", "arbitrary"), vmem_limit_bytes=VMEM_LIMIT),
        name="inproj",
    )(x, w_in, b_gate, cos, sin, w_pool, pool_scale)


def _moba_kernel(n_cast, q_ref, k_ref, v_ref, *refs):
    w_refs, o_ref, wb_refs = refs[:n_cast], refs[n_cast], refs[n_cast + 1:2 * n_cast + 1]
    khat_ref, vt_ref, qt_ref, *s_refs = refs[2 * n_cast + 1:]
    for w_ref, wb_ref in zip(w_refs, wb_refs):
        wb_ref[...] = w_ref[0].astype(BF16)

    seq = k_ref.shape[1]
    nb = seq // MOBA_BLOCK
    blk = lambda j: slice(j * MOBA_BLOCK, (j + 1) * MOBA_BLOCK)
    lane = lax.broadcasted_iota(jnp.int32, (MOBA_BLOCK, LANES), 1)

    kf = k_ref[0].astype(F32).reshape(nb, MOBA_BLOCK, LANES)
    km = jnp.sum(kf, axis=1) * (1.0 / MOBA_BLOCK)
    hi = km.astype(BF16)
    kmb = jnp.concatenate([hi, (km - hi.astype(F32)).astype(BF16)], axis=0)

    for j in range(nb):
        kj = k_ref[0, blk(j), :]
        for h in range(HEADS_PER_VREG):
            onehot = jnp.where(lane == (1 - h) * HEAD_DIM + j, 1.0, 0.0).astype(BF16)
            khat_ref[h, blk(j), :] = jnp.where((lane // HEAD_DIM) == h, kj, onehot)
        vt = v_ref[0, blk(j), :].astype(F32).T.astype(BF16)
        for h in range(HEADS_PER_VREG):
            vt_ref[h, 0:HEAD_DIM, blk(j)] = vt[h * HEAD_DIM:(h + 1) * HEAD_DIM]
            vt_ref[h, HEAD_DIM:PV_ROWS, blk(j)] = jnp.ones((PV_ROWS - HEAD_DIM, MOBA_BLOCK), BF16)
        qt_ref[:, blk(j)] = q_ref[0, blk(j), :].astype(F32).T.astype(BF16)

    dim_i = lax.broadcasted_iota(jnp.int32, (LANES, MOBA_BLOCK), 0)
    key_i = lax.broadcasted_iota(jnp.int32, (MOBA_BLOCK, MOBA_BLOCK), 0)
    qry_i = lax.broadcasted_iota(jnp.int32, (MOBA_BLOCK, MOBA_BLOCK), 1)
    causal = key_i <= qry_i
    blk_i = lax.broadcasted_iota(jnp.int32, (nb, MOBA_BLOCK), 0)
    fold = lambda t: t.reshape(MOBA_BLOCK // SUBLANES, SUBLANES, MOBA_BLOCK)

    units = [(i, h) for i in range(nb) for h in range(HEADS_PER_VREG)]
    state = {}
    sbuf = lambda u: s_refs[units.index(u) % len(s_refs)]

    def prepare(u):
        i, h = u
        in_head = (dim_i // HEAD_DIM) == h
        q = qt_ref[:, blk(i)]
        if i > MOBA_TOPK:
            g2 = _dot(kmb, jnp.where(in_head, q, jnp.zeros_like(q)))
            gate = g2[0:nb] + g2[nb:2 * nb]
            beaten = jnp.zeros((nb, MOBA_BLOCK), jnp.int32)
            for m in range(i):
                gm = gate[m:m + 1, :]
                wins = (gm > gate) | ((gm == gate) & (m < blk_i))
                beaten = beaten + jnp.where(wins, 1, 0)
            bias = jnp.where((blk_i < i) & (beaten >= MOBA_TOPK), NEG, 0.0)
            base = (1 - h) * HEAD_DIM
            parts = [bias, jnp.zeros((LANES - base - nb, MOBA_BLOCK), F32)]
            if base:
                parts = [jnp.zeros((base, MOBA_BLOCK), F32)] + parts
            spare = jnp.concatenate(parts, axis=0).astype(BF16)
        else:
            spare = jnp.zeros_like(q)
        state[u] = dict(q_aug=jnp.where(in_head, q, spare),
                        m_run=jnp.full((SUBLANES, MOBA_BLOCK), NEG, F32),
                        acc=jnp.zeros((PV_ROWS, MOBA_BLOCK), F32))

    def score_matmul(u):
        q_aug = state[u].pop("q_aug")
        n = u[0] + 1
        for lo in range(0, n, SCORE_SPLIT):
            keys = slice(lo * MOBA_BLOCK, min(lo + SCORE_SPLIT, n) * MOBA_BLOCK)
            sbuf(u)[keys, :] = _dot(khat_ref[u[1], keys, :], q_aug)

    def masked_scores(u, j):
        s = sbuf(u)[blk(j), :]
        return jnp.where(causal, s, NEG) if j == u[0] else s

    def score_step(u, j):
        st = state[u]
        st["m_run"] = jnp.maximum(st["m_run"], jnp.max(fold(masked_scores(u, j)), axis=0))

    def end_scores(u):
        state[u]["m_fin"] = jnp.max(state[u]["m_run"], axis=0, keepdims=True)

    def value_step(u, j):
        st = state[u]
        p = jnp.exp2(masked_scores(u, j) - st["m_fin"])
        st["acc"] = st["acc"] + _dot(vt_ref[u[1], :, blk(j)], p.astype(BF16))

    def finish(u):
        i, h = u
        acc = state.pop(u)["acc"]
        out = acc[0:HEAD_DIM] / acc[HEAD_DIM:HEAD_DIM + 1]
        if h == 0:
            state["head0"] = out
        else:
            both = jnp.concatenate([state.pop("head0"), out], axis=0)
            o_ref[0, blk(i), :] = both.T.astype(BF16)

    def stage_tasks(t):
        unit = lambda d: units[t + d] if 0 <= t + d < len(units) else None
        stages = []
        if unit(3):
            stages.append([functools.partial(prepare, unit(3))])
        if unit(2):
            stages.append([functools.partial(score_matmul, unit(2))])
        if unit(1):
            stages.append([functools.partial(score_step, unit(1), j) for j in range(unit(1)[0] + 1)]
                          + [functools.partial(end_scores, unit(1))])
        if unit(0):
            stages.append([functools.partial(value_step, unit(0), j) for j in range(unit(0)[0] + 1)])
        return stages

    for t in range(-3, len(units)):
        stages = stage_tasks(t)
        n = max(len(tasks) for tasks in stages)
        for step in range(n):
            for tasks in stages:
                lo, hi_ = step * len(tasks) // n, (step + 1) * len(tasks) // n
                for task in tasks[lo:hi_]:
                    task()
        if t >= 0:
            finish(units[t])


def _cast_rows(rows, steps):
    r = -(-rows // steps)
    r = -(-r // BF16_ROWS) * BF16_ROWS
    while rows % r:
        r += BF16_ROWS
    return r


def _moba(q, k, v, layer, weights):
    b, s, _ = q.shape
    n_pairs = D_ATTN // LANES
    full = pl.BlockSpec((1, s, LANES), lambda bi, p: (bi, 0, p))
    w_in_specs, w_out_specs, w_out_shapes = [], [], []
    for w in weights:
        _, rows, cols = w.shape
        r = _cast_rows(rows, b * n_pairs)
        last = rows // r - 1
        w_in_specs.append(pl.BlockSpec(
            (1, r, cols), lambda bi, p, last=last: (layer, jnp.minimum(bi * n_pairs + p, last), 0)))
        w_out_specs.append(pl.BlockSpec(
            (r, cols), lambda bi, p, last=last: (jnp.minimum(bi * n_pairs + p, last), 0)))
        w_out_shapes.append(jax.ShapeDtypeStruct((rows, cols), BF16))
    y_attn, *w_bf16 = pl.pallas_call(
        functools.partial(_moba_kernel, len(weights)),
        grid=(b, n_pairs),
        in_specs=[full, full, full] + w_in_specs,
        out_specs=[full] + w_out_specs,
        out_shape=[jax.ShapeDtypeStruct((b, s, D_ATTN), BF16)] + w_out_shapes,
        scratch_shapes=[pltpu.VMEM((HEADS_PER_VREG, s, LANES), BF16),
                        pltpu.VMEM((HEADS_PER_VREG, PV_ROWS, s), BF16),
                        pltpu.VMEM((LANES, s), BF16),
                        *[pltpu.VMEM((s, MOBA_BLOCK), F32)] * SCORE_BUFFERS],
        compiler_params=pltpu.CompilerParams(
            dimension_semantics=("arbitrary", "arbitrary"), vmem_limit_bytes=VMEM_LIMIT),
        name="moba",
    )(q, k, v, *weights)
    return y_attn, w_bf16


def _mix_ffn_kernel(alpha, x_ref, ya_ref, pm_ref, g_ref, wa_ref, wp_ref, wo_ref, l1g_ref, l1b_ref,
                    wg_ref, wu_ref, cw_ref, cb_ref, wd_ref, l2g_ref, l2b_ref, o_ref,
                    x1buf, abuf, hbuf):
    i = pl.program_id(1)
    tm = x_ref.shape[1]
    halo = SUBLANES
    groups = [slice(r * LN_ROWS, (r + 1) * LN_ROWS) for r in range(tm // LN_ROWS)]

    @pl.when(i == 0)
    def _():
        abuf[0:halo, :] = jnp.zeros((halo, D_FF), F32)

    for rows in groups:
        y_attn = _dot(ya_ref[0, rows, :], wa_ref[...])
        y_pool = _dot(pm_ref[0, rows, :], wp_ref[...])
        g_attn = g_ref[0, rows, 0:D_MODEL].astype(F32)
        g_pool = g_ref[0, rows, D_MODEL:2 * D_MODEL].astype(F32)
        mix = _dot((g_attn * y_attn + g_pool * y_pool).astype(BF16), wo_ref[...])
        x1buf[rows, :] = _layer_norm(alpha * x_ref[0, rows, :] + mix, l1g_ref[...], l1b_ref[...])

    xb = x1buf[...].astype(BF16)
    for c in range(D_FF // FF_CHUNK):
        sl = slice(c * FF_CHUNK, (c + 1) * FF_CHUNK)
        a = _dot(xb, wg_ref[:, sl])
        u = _dot(xb, wu_ref[:, sl])
        abuf[halo:halo + tm, sl] = a
        conv = a * cw_ref[CONV_WIDTH - 1:CONV_WIDTH, sl] + cb_ref[:, sl]
        for t in range(CONV_WIDTH - 1):
            back = CONV_WIDTH - 1 - t
            conv = conv + abuf[halo - back:halo - back + tm, sl] * cw_ref[t:t + 1, sl]
        abuf[0:halo, sl] = a[tm - halo:tm, :]
        gelu = 0.5 * conv * (1.0 + lax.erf(conv * (2.0 ** -0.5)))
        hbuf[:, sl] = (gelu * u).astype(BF16)
    for rows in groups:
        ffn = _dot(hbuf[rows, :], wd_ref[...])
        o_ref[0, rows, :] = _layer_norm(alpha * x1buf[rows, :] + ffn, l2g_ref[...], l2b_ref[...])


def _mix_ffn(alpha, x, y_attn, pm, gates, w_a, w_p, w_o, ln1_g, ln1_b,
             w_g, w_u, conv_w, conv_b, w_d, ln2_g, ln2_b):
    b, s, d = x.shape
    tm = ROW_TILE
    row = lambda n: pl.BlockSpec((1, tm, n), lambda bi, i: (bi, i, 0))
    consts = [w_a, w_p, w_o, ln1_g, ln1_b, w_g, w_u, conv_w, conv_b, w_d, ln2_g, ln2_b]
    return pl.pallas_call(
        functools.partial(_mix_ffn_kernel, alpha),
        grid=(b, s // tm),
        in_specs=[row(d), row(D_ATTN), row(D_POOL), row(2 * D_MODEL)]
                 + [_const_spec(c.shape) for c in consts],
        out_specs=row(d),
        out_shape=jax.ShapeDtypeStruct((b, s, d), F32),
        scratch_shapes=[pltpu.VMEM((tm, d), F32),
                        pltpu.VMEM((SUBLANES + tm, D_FF), F32),
                        pltpu.VMEM((tm, D_FF), BF16)],
        compiler_params=pltpu.CompilerParams(
            dimension_semantics=("arbitrary", "arbitrary"), vmem_limit_bytes=VMEM_LIMIT),
        name="mix_ffn",
    )(x, y_attn, pm, gates, *consts)


def _rope_tables(s):
    half = HEAD_DIM // 2
    inv_freq = 1.0 / (ROPE_THETA ** (jnp.arange(half, dtype=F32) / half))
    ang = jnp.arange(s, dtype=F32)[:, None] * inv_freq[None, :]
    cos, sin = jnp.cos(ang), jnp.sin(ang)
    cos = jnp.tile(jnp.concatenate([cos, cos], axis=-1), (1, HEADS_PER_VREG))
    sin = jnp.tile(jnp.concatenate([-sin, sin], axis=-1), (1, HEADS_PER_VREG))
    return cos, sin


def kernel(x, w_in, b_gate, w_branch_attn, w_pool, pool_scale, w_branch_pool, w_out, ln1_g, ln1_b,
           w_ffn_gate, w_ffn_up, conv_w, conv_b, w_ffn_down, ln2_g, ln2_b):
    depth = w_in.shape[0]
    s = x.shape[1]
    alpha = (2.0 * depth) ** 0.25
    cos, sin = _rope_tables(s)
    vec = lambda t: t.reshape(1, -1)
    for l in range(depth):
        q, k, v, pm, gates = _inproj(x, w_in[l].astype(BF16), vec(b_gate[l]), cos, sin,
                                     w_pool[l].astype(BF16), vec(pool_scale[l]))
        y_attn, (w_a, w_p, w_o, w_g, w_u, w_d) = _moba(
            q, k, v, l, [w_branch_attn, w_branch_pool, w_out, w_ffn_gate, w_ffn_up, w_ffn_down])
        x = _mix_ffn(alpha, x, y_attn, pm, gates, w_a, w_p, w_o, vec(ln1_g[l]), vec(ln1_b[l]),
                     w_g, w_u, conv_w[l], vec(conv_b[l]), w_d, vec(ln2_g[l]), vec(ln2_b[l]))
    return x
```

```python
import functools

import jax
import jax.numpy as jnp
from jax import lax
from jax.experimental import pallas as pl
from jax.experimental.pallas import tpu as pltpu

D_MODEL = 1024
ATTN_HEADS = 8
HEAD_DIM = 64
D_ATTN = ATTN_HEADS * HEAD_DIM
MOBA_BLOCK = 256
MOBA_TOPK = 3
ROPE_THETA = 10000.0
POOL_WINDOWS = (2, 4, 8, 16)
POOL_GROUP_DIM = 128
D_POOL = len(POOL_WINDOWS) * POOL_GROUP_DIM
D_FF = 2816
CONV_WIDTH = 3
LN_EPS = 1e-5
NEG = -1e30
Q_SCALE = HEAD_DIM ** -0.5 * 1.4426950408889634

LANES = 128
SUBLANES = 8
HEADS_PER_VREG = LANES // HEAD_DIM
BF16_ROWS = 2 * SUBLANES
PV_ROWS = HEAD_DIM + BF16_ROWS
SCORE_BUFFERS = 3
SCORE_SPLIT = 2
POOL_HALO = 16
ROW_TILE = 512
FF_CHUNK = 256
LN_ROWS = 256
VMEM_LIMIT = 52 * 1024 * 1024

F32 = jnp.float32
BF16 = jnp.bfloat16


def _dot(a, b):
    return jnp.dot(a, b, preferred_element_type=F32)


def _const_spec(shape):
    return pl.BlockSpec(shape, lambda *_: (0,) * len(shape), pipeline_mode=pl.Buffered(1))


def _layer_norm(y, g, b):
    mu = jnp.mean(y, axis=-1, keepdims=True)
    d = y - mu
    var = jnp.mean(d * d, axis=-1, keepdims=True)
    return d * lax.rsqrt(var + LN_EPS) * g + b


def _inproj_kernel(x_ref, w_ref, bg_ref, cos_ref, sin_ref, wpool_ref, pscale_ref,
                   q_ref, k_ref, v_ref, pm_ref, g_ref, ubuf, *lvl):
    i = pl.program_id(1)
    tm = x_ref.shape[1]
    xb = x_ref[0].astype(BF16)

    lane = lax.broadcasted_iota(jnp.int32, (tm, LANES), 1)
    first_half = (lane % HEAD_DIM) < (HEAD_DIM // 2)
    cos = cos_ref[...]
    sin = sin_ref[...]

    def rope(t):
        partner = jnp.where(first_half, pltpu.roll(t, LANES - HEAD_DIM // 2, 1),
                            pltpu.roll(t, HEAD_DIM // 2, 1))
        return t * cos + partner * sin

    pad, cur, end = SUBLANES, SUBLANES + POOL_HALO, SUBLANES + POOL_HALO + tm

    @pl.when(i == 0)
    def _():
        ubuf[0:cur, :] = jnp.zeros((cur, D_POOL), F32)
        for level in lvl:
            level[0:pad, :] = jnp.zeros((pad, level.shape[1]), F32)

    o3 = 3 * D_ATTN
    ubuf[cur:end, :] = _dot(xb, w_ref[:, o3:o3 + D_POOL])

    zq = _dot(xb, w_ref[:, 0:D_ATTN])
    for c in range(D_ATTN // LANES):
        sl = slice(c * LANES, (c + 1) * LANES)
        q_ref[0, :, sl] = (rope(zq[:, sl]) * Q_SCALE).astype(BF16)
    zk = _dot(xb, w_ref[:, D_ATTN:2 * D_ATTN])
    for c in range(D_ATTN // LANES):
        sl = slice(c * LANES, (c + 1) * LANES)
        k_ref[0, :, sl] = rope(zk[:, sl]).astype(BF16)
    v_ref[0] = _dot(xb, w_ref[:, 2 * D_ATTN:3 * D_ATTN]).astype(BF16)

    o4 = o3 + D_POOL
    for c in range(2 * D_MODEL // D_ATTN):
        sl = slice(c * D_ATTN, (c + 1) * D_ATTN)
        zg = _dot(xb, w_ref[:, o4 + c * D_ATTN:o4 + (c + 1) * D_ATTN]) + bg_ref[:, sl]
        g_ref[0, :, sl] = jax.nn.sigmoid(zg).astype(BF16)

    pos = i * tm + lax.broadcasted_iota(jnp.int32, (tm, 1), 0)
    prev = ubuf
    for g, w in enumerate(POOL_WINDOWS):
        sl = slice(g * POOL_GROUP_DIM, (g + 1) * POOL_GROUP_DIM)
        sums = prev[pad:end, :] + prev[pad - w // 2:end - w // 2, :]
        if g + 1 < len(POOL_WINDOWS):
            lvl[g][pad:end, :] = sums[:, POOL_GROUP_DIM:]
            prev = lvl[g]
        inv_count = 1.0 / jnp.minimum(pos + 1, w).astype(F32)
        pooled = sums[POOL_HALO:, 0:POOL_GROUP_DIM] * inv_count - ubuf[cur:end, sl]
        mixed = _dot(pooled.astype(BF16), wpool_ref[g]) * pscale_ref[:, sl]
        pm_ref[0, :, sl] = mixed.astype(BF16)
    ubuf[pad:cur, :] = ubuf[end - POOL_HALO:end, :]


def _inproj(x, w_in, b_gate, cos, sin, w_pool, pool_scale):
    b, s, d = x.shape
    tm = 2 * ROW_TILE
    n_in = w_in.shape[1]
    assert all(w == 2 ** (g + 1) for g, w in enumerate(POOL_WINDOWS)) and POOL_WINDOWS[-1] <= POOL_HALO
    row = lambda n: pl.BlockSpec((1, tm, n), lambda bi, i: (bi, i, 0))
    return pl.pallas_call(
        _inproj_kernel,
        grid=(b, s // tm),
        in_specs=[
            row(d),
            _const_spec((d, n_in)),
            _const_spec((1, 2 * D_MODEL)),
            pl.BlockSpec((tm, LANES), lambda bi, i: (i, 0)),
            pl.BlockSpec((tm, LANES), lambda bi, i: (i, 0)),
            _const_spec(w_pool.shape),
            _const_spec((1, D_POOL)),
        ],
        out_specs=[row(D_ATTN), row(D_ATTN), row(D_ATTN), row(D_POOL), row(2 * D_MODEL)],
        out_shape=[
            jax.ShapeDtypeStruct((b, s, D_ATTN), BF16),
            jax.ShapeDtypeStruct((b, s, D_ATTN), BF16),
            jax.ShapeDtypeStruct((b, s, D_ATTN), BF16),
            jax.ShapeDtypeStruct((b, s, D_POOL), BF16),
            jax.ShapeDtypeStruct((b, s, 2 * D_MODEL), BF16),
        ],
        scratch_shapes=[pltpu.VMEM((SUBLANES + POOL_HALO + tm, D_POOL - g * POOL_GROUP_DIM), F32)
                        for g in range(len(POOL_WINDOWS))],
        compiler_params=pltpu.CompilerParams(
            dimension_semantics=("arbitrary", "arbitrary"), vmem_limit_bytes=VMEM_LIMIT),
        name="inproj",
    )(x, w_in, b_gate, cos, sin, w_pool, pool_scale)


def _moba_kernel(n_cast, q_ref, k_ref, v_ref, *refs):
    w_refs, o_ref, wb_refs = refs[:n_cast], refs[n_cast], refs[n_cast + 1:2 * n_cast + 1]
    khat_ref, vt_ref, qt_ref, *s_refs = refs[2 * n_cast + 1:]
    for w_ref, wb_ref in zip(w_refs, wb_refs):
        wb_ref[...] = w_ref[0].astype(BF16)

    seq = k_ref.shape[1]
    nb = seq // MOBA_BLOCK
    blk = lambda j: slice(j * MOBA_BLOCK, (j + 1) * MOBA_BLOCK)
    lane = lax.broadcasted_iota(jnp.int32, (MOBA_BLOCK, LANES), 1)

    kf = k_ref[0].astype(F32).reshape(nb, MOBA_BLOCK, LANES)
    km = jnp.sum(kf, axis=1) * (1.0 / MOBA_BLOCK)
    hi = km.astype(BF16)
    kmb = jnp.concatenate([hi, (km - hi.astype(F32)).astype(BF16)], axis=0)

    for j in range(nb):
        kj = k_ref[0, blk(j), :]
        for h in range(HEADS_PER_VREG):
            onehot = jnp.where(lane == (1 - h) * HEAD_DIM + j, 1.0, 0.0).astype(BF16)
            khat_ref[h, blk(j), :] = jnp.where((lane // HEAD_DIM) == h, kj, onehot)
        vt = v_ref[0, blk(j), :].astype(F32).T.astype(BF16)
        for h in range(HEADS_PER_VREG):
            vt_ref[h, 0:HEAD_DIM, blk(j)] = vt[h * HEAD_DIM:(h + 1) * HEAD_DIM]
            vt_ref[h, HEAD_DIM:PV_ROWS, blk(j)] = jnp.ones((PV_ROWS - HEAD_DIM, MOBA_BLOCK), BF16)
        qt_ref[:, blk(j)] = q_ref[0, blk(j), :].astype(F32).T.astype(BF16)

    dim_i = lax.broadcasted_iota(jnp.int32, (LANES, MOBA_BLOCK), 0)
    key_i = lax.broadcasted_iota(jnp.int32, (MOBA_BLOCK, MOBA_BLOCK), 0)
    qry_i = lax.broadcasted_iota(jnp.int32, (MOBA_BLOCK, MOBA_BLOCK), 1)
    causal = key_i <= qry_i
    blk_i = lax.broadcasted_iota(jnp.int32, (nb, MOBA_BLOCK), 0)
    fold = lambda t: t.reshape(MOBA_BLOCK // SUBLANES, SUBLANES, MOBA_BLOCK)

    units = [(i, h) for i in range(nb) for h in range(HEADS_PER_VREG)]
    state = {}
    sbuf = lambda u: s_refs[units.index(u) % len(s_refs)]

    def prepare(u):
        i, h = u
        in_head = (dim_i // HEAD_DIM) == h
        q = qt_ref[:, blk(i)]
        if i > MOBA_TOPK:
            g2 = _dot(kmb, jnp.where(in_head, q, jnp.zeros_like(q)))
            gate = g2[0:nb] + g2[nb:2 * nb]
            beaten = jnp.zeros((nb, MOBA_BLOCK), jnp.int32)
            for m in range(i):
                gm = gate[m:m + 1, :]
                wins = (gm > gate) | ((gm == gate) & (m < blk_i))
                beaten = beaten + jnp.where(wins, 1, 0)
            bias = jnp.where((blk_i < i) & (beaten >= MOBA_TOPK), NEG, 0.0)
            base = (1 - h) * HEAD_DIM
            parts = [bias, jnp.zeros((LANES - base - nb, MOBA_BLOCK), F32)]
            if base:
                parts = [jnp.zeros((base, MOBA_BLOCK), F32)] + parts
            spare = jnp.concatenate(parts, axis=0).astype(BF16)
        else:
            spare = jnp.zeros_like(q)
        state[u] = dict(q_aug=jnp.where(in_head, q, spare),
                        m_run=jnp.full((SUBLANES, MOBA_BLOCK), NEG, F32),
                        acc=jnp.zeros((PV_ROWS, MOBA_BLOCK), F32))

    def score_matmul(u):
        q_aug = state[u].pop("q_aug")
        n = u[0] + 1
        for lo in range(0, n, SCORE_SPLIT):
            keys = slice(lo * MOBA_BLOCK, min(lo + SCORE_SPLIT, n) * MOBA_BLOCK)
            sbuf(u)[keys, :] = _dot(khat_ref[u[1], keys, :], q_aug)

    def masked_scores(u, j):
        s = sbuf(u)[blk(j), :]
        return jnp.where(causal, s, NEG) if j == u[0] else s

    def score_step(u, j):
        st = state[u]
        st["m_run"] = jnp.maximum(st["m_run"], jnp.max(fold(masked_scores(u, j)), axis=0))

    def end_scores(u):
        state[u]["m_fin"] = jnp.max(state[u]["m_run"], axis=0, keepdims=True)

    def value_step(u, j):
        st = state[u]
        p = jnp.exp2(masked_scores(u, j) - st["m_fin"])
        st["acc"] = st["acc"] + _dot(vt_ref[u[1], :, blk(j)], p.astype(BF16))

    def finish(u):
        i, h = u
        acc = state.pop(u)["acc"]
        out = acc[0:HEAD_DIM] / acc[HEAD_DIM:HEAD_DIM + 1]
        if h == 0:
            state["head0"] = out
        else:
            both = jnp.concatenate([state.pop("head0"), out], axis=0)
            o_ref[0, blk(i), :] = both.T.astype(BF16)

    def stage_tasks(t):
        unit = lambda d: units[t + d] if 0 <= t + d < len(units) else None
        stages = []
        if unit(3):
            stages.append([functools.partial(prepare, unit(3))])
        if unit(2):
            stages.append([functools.partial(score_matmul, unit(2))])
        if unit(1):
            stages.append([functools.partial(score_step, unit(1), j) for j in range(unit(1)[0] + 1)]
                          + [functools.partial(end_scores, unit(1))])
        if unit(0):
            stages.append([functools.partial(value_step, unit(0), j) for j in range(unit(0)[0] + 1)])
        return stages

    for t in range(-3, len(units)):
        stages = stage_tasks(t)
        n = max(len(tasks) for tasks in stages)
        for step in range(n):
            for tasks in stages:
                lo, hi_ = step * len(tasks) // n, (step + 1) * len(tasks) // n
                for task in tasks[lo:hi_]:
                    task()
        if t >= 0:
            finish(units[t])


def _cast_rows(rows, steps):
    r = -(-rows // steps)
    r = -(-r // BF16_ROWS) * BF16_ROWS
    while rows % r:
        r += BF16_ROWS
    return r


def _moba(q, k, v, layer, weights):
    b, s, _ = q.shape
    n_pairs = D_ATTN // LANES
    full = pl.BlockSpec((1, s, LANES), lambda bi, p: (bi, 0, p))
    w_in_specs, w_out_specs, w_out_shapes = [], [], []
    for w in weights:
        _, rows, cols = w.shape
        r = _cast_rows(rows, b * n_pairs)
        last = rows // r - 1
        w_in_specs.append(pl.BlockSpec(
            (1, r, cols), lambda bi, p, last=last: (layer, jnp.minimum(bi * n_pairs + p, last), 0)))
        w_out_specs.append(pl.BlockSpec(
            (r, cols), lambda bi, p, last=last: (jnp.minimum(bi * n_pairs + p, last), 0)))
        w_out_shapes.append(jax.ShapeDtypeStruct((rows, cols), BF16))
    y_attn, *w_bf16 = pl.pallas_call(
        functools.partial(_moba_kernel, len(weights)),
        grid=(b, n_pairs),
        in_specs=[full, full, full] + w_in_specs,
        out_specs=[full] + w_out_specs,
        out_shape=[jax.ShapeDtypeStruct((b, s, D_ATTN), BF16)] + w_out_shapes,
        scratch_shapes=[pltpu.VMEM((HEADS_PER_VREG, s, LANES), BF16),
                        pltpu.VMEM((HEADS_PER_VREG, PV_ROWS, s), BF16),
                        pltpu.VMEM((LANES, s), BF16),
                        *[pltpu.VMEM((s, MOBA_BLOCK), F32)] * SCORE_BUFFERS],
        compiler_params=pltpu.CompilerParams(
            dimension_semantics=("arbitrary", "arbitrary"), vmem_limit_bytes=VMEM_LIMIT),
        name="moba",
    )(q, k, v, *weights)
    return y_attn, w_bf16


def _mix_ffn_kernel(alpha, x_ref, ya_ref, pm_ref, g_ref, wa_ref, wp_ref, wo_ref, l1g_ref, l1b_ref,
                    wg_ref, wu_ref, cw_ref, cb_ref, wd_ref, l2g_ref, l2b_ref, o_ref,
                    x1buf, abuf, hbuf):
    i = pl.program_id(1)
    tm = x_ref.shape[1]
    halo = SUBLANES
    groups = [slice(r * LN_ROWS, (r + 1) * LN_ROWS) for r in range(tm // LN_ROWS)]

    @pl.when(i == 0)
    def _():
        abuf[0:halo, :] = jnp.zeros((halo, D_FF), F32)

    for rows in groups:
        y_attn = _dot(ya_ref[0, rows, :], wa_ref[...])
        y_pool = _dot(pm_ref[0, rows, :], wp_ref[...])
        g_attn = g_ref[0, rows, 0:D_MODEL].astype(F32)
        g_pool = g_ref[0, rows, D_MODEL:2 * D_MODEL].astype(F32)
        mix = _dot((g_attn * y_attn + g_pool * y_pool).astype(BF16), wo_ref[...])
        x1buf[rows, :] = _layer_norm(alpha * x_ref[0, rows, :] + mix, l1g_ref[...], l1b_ref[...])

    xb = x1buf[...].astype(BF16)
    for c in range(D_FF // FF_CHUNK):
        sl = slice(c * FF_CHUNK, (c + 1) * FF_CHUNK)
        if c == 0:
            parts = [(_dot(xg, wg_ref[:, sl]), _dot(xg, wu_ref[:, sl]))
                     for xg in (x1buf[rows, :].astype(BF16) for rows in groups)]
            a = jnp.concatenate([p[0] for p in parts])
            u = jnp.concatenate([p[1] for p in parts])
        else:
            a = _dot(xb, wg_ref[:, sl])
            u = _dot(xb, wu_ref[:, sl])
        abuf[halo:halo + tm, sl] = a
        conv = a * cw_ref[CONV_WIDTH - 1:CONV_WIDTH, sl] + cb_ref[:, sl]
        for t in range(CONV_WIDTH - 1):
            back = CONV_WIDTH - 1 - t
            conv = conv + abuf[halo - back:halo - back + tm, sl] * cw_ref[t:t + 1, sl]
        abuf[0:halo, sl] = a[tm - halo:tm, :]
        gelu = 0.5 * conv * (1.0 + lax.erf(conv * (2.0 ** -0.5)))
        hbuf[:, sl] = (gelu * u).astype(BF16)
    for rows in groups:
        ffn = _dot(hbuf[rows, :], wd_ref[...])
        o_ref[0, rows, :] = _layer_norm(alpha * x1buf[rows, :] + ffn, l2g_ref[...], l2b_ref[...])


def _mix_ffn(alpha, x, y_attn, pm, gates, w_a, w_p, w_o, ln1_g, ln1_b,
             w_g, w_u, conv_w, conv_b, w_d, ln2_g, ln2_b):
    b, s, d = x.shape
    tm = ROW_TILE
    row = lambda n: pl.BlockSpec((1, tm, n), lambda bi, i: (bi, i, 0))
    consts = [w_a, w_p, w_o, ln1_g, ln1_b, w_g, w_u, conv_w, conv_b, w_d, ln2_g, ln2_b]
    return pl.pallas_call(
        functools.partial(_mix_ffn_kernel, alpha),
        grid=(b, s // tm),
        in_specs=[row(d), row(D_ATTN), row(D_POOL), row(2 * D_MODEL)]
                 + [_const_spec(c.shape) for c in consts],
        out_specs=row(d),
        out_shape=jax.ShapeDtypeStruct((b, s, d), F32),
        scratch_shapes=[pltpu.VMEM((tm, d), F32),
                        pltpu.VMEM((SUBLANES + tm, D_FF), F32),
                        pltpu.VMEM((tm, D_FF), BF16)],
        compiler_params=pltpu.CompilerParams(
            dimension_semantics=("arbitrary", "arbitrary"), vmem_limit_bytes=VMEM_LIMIT),
        name="mix_ffn",
    )(x, y_attn, pm, gates, *consts)


def _rope_tables(s):
    half = HEAD_DIM // 2
    inv_freq = 1.0 / (ROPE_THETA ** (jnp.arange(half, dtype=F32) / half))
    ang = jnp.arange(s, dtype=F32)[:, None] * inv_freq[None, :]
    cos, sin = jnp.cos(ang), jnp.sin(ang)
    cos = jnp.tile(jnp.concatenate([cos, cos], axis=-1), (1, HEADS_PER_VREG))
    sin = jnp.tile(jnp.concatenate([-sin, sin], axis=-1), (1, HEADS_PER_VREG))
    return cos, sin


def kernel(x, w_in, b_gate, w_branch_attn, w_pool, pool_scale, w_branch_pool, w_out, ln1_g, ln1_b,
           w_ffn_gate, w_ffn_up, conv_w, conv_b, w_ffn_down, ln2_g, ln2_b):
    depth = w_in.shape[0]
    s = x.shape[1]
    alpha = (2.0 * depth) ** 0.25
    cos, sin = _rope_tables(s)
    vec = lambda t: t.reshape(1, -1)
    for l in range(depth):
        q, k, v, pm, gates = _inproj(x, w_in[l].astype(BF16), vec(b_gate[l]), cos, sin,
                                     w_pool[l].astype(BF16), vec(pool_scale[l]))
        y_attn, (w_a, w_p, w_o, w_g, w_u, w_d) = _moba(
            q, k, v, l, [w_branch_attn, w_branch_pool, w_out, w_ffn_gate, w_ffn_up, w_ffn_down])
        x = _mix_ffn(alpha, x, y_attn, pm, gates, w_a, w_p, w_o, vec(ln1_g[l]), vec(ln1_b[l]),
                     w_g, w_u, conv_w[l], vec(conv_b[l]), w_d, vec(ln2_g[l]), vec(ln2_b[l]))
    return x
```

```python
import functools

import jax
import jax.numpy as jnp
from jax import lax
from jax.experimental import pallas as pl
from jax.experimental.pallas import tpu as pltpu

D_MODEL = 1024
ATTN_HEADS = 8
HEAD_DIM = 64
D_ATTN = ATTN_HEADS * HEAD_DIM
MOBA_BLOCK = 256
MOBA_TOPK = 3
ROPE_THETA = 10000.0
POOL_WINDOWS = (2, 4, 8, 16)
POOL_GROUP_DIM = 128
D_POOL = len(POOL_WINDOWS) * POOL_GROUP_DIM
D_FF = 2816
CONV_WIDTH = 3
LN_EPS = 1e-5
NEG = -1e30
Q_SCALE = HEAD_DIM ** -0.5 * 1.4426950408889634

LANES = 128
SUBLANES = 8
HEADS_PER_VREG = LANES // HEAD_DIM
BF16_ROWS = 2 * SUBLANES
PV_ROWS = HEAD_DIM + BF16_ROWS
SCORE_BUFFERS = 3
SCORE_SPLIT = 2
POOL_HALO = 16
ROW_TILE = 512
FF_CHUNK = 256
LN_ROWS = 256
VMEM_LIMIT = 52 * 1024 * 1024

F32 = jnp.float32
BF16 = jnp.bfloat16


def _dot(a, b):
    return jnp.dot(a, b, preferred_element_type=F32)


def _const_spec(shape):
    return pl.BlockSpec(shape, lambda *_: (0,) * len(shape), pipeline_mode=pl.Buffered(1))


def _layer_norm(y, g, b):
    mu = jnp.mean(y, axis=-1, keepdims=True)
    d = y - mu
    var = jnp.mean(d * d, axis=-1, keepdims=True)
    return d * lax.rsqrt(var + LN_EPS) * g + b


def _inproj_kernel(x_ref, wf_ref, bg_ref, cos_ref, sin_ref, wpool_ref, pscale_ref,
                   q_ref, k_ref, v_ref, pm_ref, g_ref, w_ref, ubuf, *lvl):
    i = pl.program_id(1)
    tm = x_ref.shape[1]

    @pl.when((pl.program_id(0) == 0) & (i == 0))
    def _():
        for c in range(0, w_ref.shape[1], D_ATTN):
            w_ref[:, c:c + D_ATTN] = wf_ref[0, :, c:c + D_ATTN].astype(BF16)

    xb = x_ref[0].astype(BF16)

    lane = lax.broadcasted_iota(jnp.int32, (tm, LANES), 1)
    first_half = (lane % HEAD_DIM) < (HEAD_DIM // 2)
    cos = cos_ref[...]
    sin = sin_ref[...]

    def rope(t):
        partner = jnp.where(first_half, pltpu.roll(t, LANES - HEAD_DIM // 2, 1),
                            pltpu.roll(t, HEAD_DIM // 2, 1))
        return t * cos + partner * sin

    pad, cur, end = SUBLANES, SUBLANES + POOL_HALO, SUBLANES + POOL_HALO + tm

    @pl.when(i == 0)
    def _():
        ubuf[0:cur, :] = jnp.zeros((cur, D_POOL), F32)
        for level in lvl:
            level[0:pad, :] = jnp.zeros((pad, level.shape[1]), F32)

    o3 = 3 * D_ATTN
    ubuf[cur:end, :] = _dot(xb, w_ref[:, o3:o3 + D_POOL])

    zq = _dot(xb, w_ref[:, 0:D_ATTN])
    for c in range(D_ATTN // LANES):
        sl = slice(c * LANES, (c + 1) * LANES)
        q_ref[0, :, sl] = (rope(zq[:, sl]) * Q_SCALE).astype(BF16)
    zk = _dot(xb, w_ref[:, D_ATTN:2 * D_ATTN])
    for c in range(D_ATTN // LANES):
        sl = slice(c * LANES, (c + 1) * LANES)
        k_ref[0, :, sl] = rope(zk[:, sl]).astype(BF16)
    v_ref[0] = _dot(xb, w_ref[:, 2 * D_ATTN:3 * D_ATTN]).astype(BF16)

    o4 = o3 + D_POOL
    for c in range(2 * D_MODEL // D_ATTN):
        sl = slice(c * D_ATTN, (c + 1) * D_ATTN)
        zg = _dot(xb, w_ref[:, o4 + c * D_ATTN:o4 + (c + 1) * D_ATTN]) + bg_ref[:, sl]
        g_ref[0, :, sl] = jax.nn.sigmoid(zg).astype(BF16)

    pos = i * tm + lax.broadcasted_iota(jnp.int32, (tm, 1), 0)
    prev = ubuf
    for g, w in enumerate(POOL_WINDOWS):
        sl = slice(g * POOL_GROUP_DIM, (g + 1) * POOL_GROUP_DIM)
        sums = prev[pad:end, :] + prev[pad - w // 2:end - w // 2, :]
        if g + 1 < len(POOL_WINDOWS):
            lvl[g][pad:end, :] = sums[:, POOL_GROUP_DIM:]
            prev = lvl[g]
        inv_count = 1.0 / jnp.minimum(pos + 1, w).astype(F32)
        pooled = sums[POOL_HALO:, 0:POOL_GROUP_DIM] * inv_count - ubuf[cur:end, sl]
        mixed = _dot(pooled.astype(BF16), wpool_ref[g]) * pscale_ref[:, sl]
        pm_ref[0, :, sl] = mixed.astype(BF16)
    ubuf[pad:cur, :] = ubuf[end - POOL_HALO:end, :]


def _inproj(x, layer, w_in, b_gate, cos, sin, w_pool, pool_scale):
    b, s, d = x.shape
    tm = ROW_TILE
    n_in = w_in.shape[2]
    assert all(w == 2 ** (g + 1) for g, w in enumerate(POOL_WINDOWS)) and POOL_WINDOWS[-1] <= POOL_HALO
    row = lambda n: pl.BlockSpec((1, tm, n), lambda bi, i: (bi, i, 0))
    return pl.pallas_call(
        _inproj_kernel,
        grid=(b, s // tm),
        in_specs=[
            row(d),
            pl.BlockSpec((1, d, n_in), lambda *_: (layer, 0, 0), pipeline_mode=pl.Buffered(1)),
            _const_spec((1, 2 * D_MODEL)),
            pl.BlockSpec((tm, LANES), lambda bi, i: (i, 0)),
            pl.BlockSpec((tm, LANES), lambda bi, i: (i, 0)),
            _const_spec(w_pool.shape),
            _const_spec((1, D_POOL)),
        ],
        out_specs=[row(D_ATTN), row(D_ATTN), row(D_ATTN), row(D_POOL), row(2 * D_MODEL)],
        out_shape=[
            jax.ShapeDtypeStruct((b, s, D_ATTN), BF16),
            jax.ShapeDtypeStruct((b, s, D_ATTN), BF16),
            jax.ShapeDtypeStruct((b, s, D_ATTN), BF16),
            jax.ShapeDtypeStruct((b, s, D_POOL), BF16),
            jax.ShapeDtypeStruct((b, s, 2 * D_MODEL), BF16),
        ],
        scratch_shapes=[pltpu.VMEM((d, n_in), BF16)]
                       + [pltpu.VMEM((SUBLANES + POOL_HALO + tm, D_POOL - g * POOL_GROUP_DIM), F32)
                          for g in range(len(POOL_WINDOWS))],
        compiler_params=pltpu.CompilerParams(
            dimension_semantics=("arbitrary", "arbitrary"), vmem_limit_bytes=VMEM_LIMIT),
        name="inproj",
    )(x, w_in, b_gate, cos, sin, w_pool, pool_scale)


def _moba_kernel(n_cast, q_ref, k_ref, v_ref, *refs):
    w_refs, o_ref, wb_refs = refs[:n_cast], refs[n_cast], refs[n_cast + 1:2 * n_cast + 1]
    khat_ref, vt_ref, qt_ref, *s_refs = refs[2 * n_cast + 1:]
    for w_ref, wb_ref in zip(w_refs, wb_refs):
        wb_ref[...] = w_ref[0].astype(BF16)

    seq = k_ref.shape[1]
    nb = seq // MOBA_BLOCK
    blk = lambda j: slice(j * MOBA_BLOCK, (j + 1) * MOBA_BLOCK)
    lane = lax.broadcasted_iota(jnp.int32, (MOBA_BLOCK, LANES), 1)

    kf = k_ref[0].astype(F32).reshape(nb, MOBA_BLOCK, LANES)
    km = jnp.sum(kf, axis=1) * (1.0 / MOBA_BLOCK)
    hi = km.astype(BF16)
    kmb = jnp.concatenate([hi, (km - hi.astype(F32)).astype(BF16)], axis=0)

    for j in range(nb):
        kj = k_ref[0, blk(j), :]
        for h in range(HEADS_PER_VREG):
            onehot = jnp.where(lane == (1 - h) * HEAD_DIM + j, 1.0, 0.0).astype(BF16)
            khat_ref[h, blk(j), :] = jnp.where((lane // HEAD_DIM) == h, kj, onehot)
        vt = v_ref[0, blk(j), :].astype(F32).T.astype(BF16)
        for h in range(HEADS_PER_VREG):
            vt_ref[h, 0:HEAD_DIM, blk(j)] = vt[h * HEAD_DIM:(h + 1) * HEAD_DIM]
            vt_ref[h, HEAD_DIM:PV_ROWS, blk(j)] = jnp.ones((PV_ROWS - HEAD_DIM, MOBA_BLOCK), BF16)
        qt_ref[:, blk(j)] = q_ref[0, blk(j), :].astype(F32).T.astype(BF16)

    dim_i = lax.broadcasted_iota(jnp.int32, (LANES, MOBA_BLOCK), 0)
    key_i = lax.broadcasted_iota(jnp.int32, (MOBA_BLOCK, MOBA_BLOCK), 0)
    qry_i = lax.broadcasted_iota(jnp.int32, (MOBA_BLOCK, MOBA_BLOCK), 1)
    causal = key_i <= qry_i
    blk_i = lax.broadcasted_iota(jnp.int32, (nb, MOBA_BLOCK), 0)
    fold = lambda t: t.reshape(MOBA_BLOCK // SUBLANES, SUBLANES, MOBA_BLOCK)

    units = [(i, h) for i in range(nb) for h in range(HEADS_PER_VREG)]
    state = {}
    sbuf = lambda u: s_refs[units.index(u) % len(s_refs)]

    def prepare(u):
        i, h = u
        in_head = (dim_i // HEAD_DIM) == h
        q = qt_ref[:, blk(i)]
        if i > MOBA_TOPK:
            g2 = _dot(kmb, jnp.where(in_head, q, jnp.zeros_like(q)))
            gate = g2[0:nb] + g2[nb:2 * nb]
            beaten = jnp.zeros((nb, MOBA_BLOCK), jnp.int32)
            for m in range(i):
                gm = gate[m:m + 1, :]
                wins = (gm > gate) | ((gm == gate) & (m < blk_i))
                beaten = beaten + jnp.where(wins, 1, 0)
            bias = jnp.where((blk_i < i) & (beaten >= MOBA_TOPK), NEG, 0.0)
            base = (1 - h) * HEAD_DIM
            parts = [bias, jnp.zeros((LANES - base - nb, MOBA_BLOCK), F32)]
            if base:
                parts = [jnp.zeros((base, MOBA_BLOCK), F32)] + parts
            spare = jnp.concatenate(parts, axis=0).astype(BF16)
        else:
            spare = jnp.zeros_like(q)
        state[u] = dict(q_aug=jnp.where(in_head, q, spare),
                        m_run=jnp.full((SUBLANES, MOBA_BLOCK), NEG, F32),
                        acc=jnp.zeros((PV_ROWS, MOBA_BLOCK), F32))

    def score_matmul(u):
        q_aug = state[u].pop("q_aug")
        n = u[0] + 1
        for lo in range(0, n, SCORE_SPLIT):
            keys = slice(lo * MOBA_BLOCK, min(lo + SCORE_SPLIT, n) * MOBA_BLOCK)
            sbuf(u)[keys, :] = _dot(khat_ref[u[1], keys, :], q_aug)

    def masked_scores(u, j):
        s = sbuf(u)[blk(j), :]
        return jnp.where(causal, s, NEG) if j == u[0] else s

    def score_step(u, j):
        st = state[u]
        st["m_run"] = jnp.maximum(st["m_run"], jnp.max(fold(masked_scores(u, j)), axis=0))

    def end_scores(u):
        state[u]["m_fin"] = jnp.max(state[u]["m_run"], axis=0, keepdims=True)

    def value_step(u, j):
        st = state[u]
        p = jnp.exp2(masked_scores(u, j) - st["m_fin"])
        st["acc"] = st["acc"] + _dot(vt_ref[u[1], :, blk(j)], p.astype(BF16))

    def finish(u):
        i, h = u
        acc = state.pop(u)["acc"]
        out = acc[0:HEAD_DIM] / acc[HEAD_DIM:HEAD_DIM + 1]
        if h == 0:
            state["head0"] = out
        else:
            both = jnp.concatenate([state.pop("head0"), out], axis=0)
            o_ref[0, blk(i), :] = both.T.astype(BF16)

    def stage_tasks(t):
        unit = lambda d: units[t + d] if 0 <= t + d < len(units) else None
        stages = []
        if unit(3):
            stages.append([functools.partial(prepare, unit(3))])
        if unit(2):
            stages.append([functools.partial(score_matmul, unit(2))])
        if unit(1):
            stages.append([functools.partial(score_step, unit(1), j) for j in range(unit(1)[0] + 1)]
                          + [functools.partial(end_scores, unit(1))])
        if unit(0):
            stages.append([functools.partial(value_step, unit(0), j) for j in range(unit(0)[0] + 1)])
        return stages

    for t in range(-3, len(units)):
        stages = stage_tasks(t)
        n = max(len(tasks) for tasks in stages)
        for step in range(n):
            for tasks in stages:
                lo, hi_ = step * len(tasks) // n, (step + 1) * len(tasks) // n
                for task in tasks[lo:hi_]:
                    task()
        if t >= 0:
            finish(units[t])


def _cast_rows(rows, steps):
    r = -(-rows // steps)
    r = -(-r // BF16_ROWS) * BF16_ROWS
    while rows % r:
        r += BF16_ROWS
    return r


def _moba(q, k, v, layer, weights):
    b, s, _ = q.shape
    n_pairs = D_ATTN // LANES
    full = pl.BlockSpec((1, s, LANES), lambda bi, p: (bi, 0, p))
    w_in_specs, w_out_specs, w_out_shapes = [], [], []
    for w in weights:
        _, rows, cols = w.shape
        r = _cast_rows(rows, b * n_pairs)
        last = rows // r - 1
        w_in_specs.append(pl.BlockSpec(
            (1, r, cols), lambda bi, p, last=last: (layer, jnp.minimum(bi * n_pairs + p, last), 0)))
        w_out_specs.append(pl.BlockSpec(
            (r, cols), lambda bi, p, last=last: (jnp.minimum(bi * n_pairs + p, last), 0)))
        w_out_shapes.append(jax.ShapeDtypeStruct((rows, cols), BF16))
    y_attn, *w_bf16 = pl.pallas_call(
        functools.partial(_moba_kernel, len(weights)),
        grid=(b, n_pairs),
        in_specs=[full, full, full] + w_in_specs,
        out_specs=[full] + w_out_specs,
        out_shape=[jax.ShapeDtypeStruct((b, s, D_ATTN), BF16)] + w_out_shapes,
        scratch_shapes=[pltpu.VMEM((HEADS_PER_VREG, s, LANES), BF16),
                        pltpu.VMEM((HEADS_PER_VREG, PV_ROWS, s), BF16),
                        pltpu.VMEM((LANES, s), BF16),
                        *[pltpu.VMEM((s, MOBA_BLOCK), F32)] * SCORE_BUFFERS],
        compiler_params=pltpu.CompilerParams(
            dimension_semantics=("arbitrary", "arbitrary"), vmem_limit_bytes=VMEM_LIMIT),
        name="moba",
    )(q, k, v, *weights)
    return y_attn, w_bf16


def _mix_ffn_kernel(alpha, x_ref, ya_ref, pm_ref, g_ref, wa_ref, wp_ref, wo_ref, l1g_ref, l1b_ref,
                    wg_ref, wu_ref, cw_ref, cb_ref, wd_ref, l2g_ref, l2b_ref, o_ref,
                    x1buf, abuf, hbuf):
    i = pl.program_id(1)
    tm = x_ref.shape[1]
    halo = SUBLANES
    groups = [slice(r * LN_ROWS, (r + 1) * LN_ROWS) for r in range(tm // LN_ROWS)]

    @pl.when(i == 0)
    def _():
        abuf[0:halo, :] = jnp.zeros((halo, D_FF), F32)

    for rows in groups:
        y_attn = _dot(ya_ref[0, rows, :], wa_ref[...])
        y_pool = _dot(pm_ref[0, rows, :], wp_ref[...])
        g_attn = g_ref[0, rows, 0:D_MODEL].astype(F32)
        g_pool = g_ref[0, rows, D_MODEL:2 * D_MODEL].astype(F32)
        mix = _dot((g_attn * y_attn + g_pool * y_pool).astype(BF16), wo_ref[...])
        x1buf[rows, :] = _layer_norm(alpha * x_ref[0, rows, :] + mix, l1g_ref[...], l1b_ref[...])

    xb = x1buf[...].astype(BF16)
    for c in range(D_FF // FF_CHUNK):
        sl = slice(c * FF_CHUNK, (c + 1) * FF_CHUNK)
        if c == 0:
            parts = [(_dot(xg, wg_ref[:, sl]), _dot(xg, wu_ref[:, sl]))
                     for xg in (x1buf[rows, :].astype(BF16) for rows in groups)]
            a = jnp.concatenate([p[0] for p in parts])
            u = jnp.concatenate([p[1] for p in parts])
        else:
            a = _dot(xb, wg_ref[:, sl])
            u = _dot(xb, wu_ref[:, sl])
        abuf[halo:halo + tm, sl] = a
        conv = a * cw_ref[CONV_WIDTH - 1:CONV_WIDTH, sl] + cb_ref[:, sl]
        for t in range(CONV_WIDTH - 1):
            back = CONV_WIDTH - 1 - t
            conv = conv + abuf[halo - back:halo - back + tm, sl] * cw_ref[t:t + 1, sl]
        abuf[0:halo, sl] = a[tm - halo:tm, :]
        gelu = 0.5 * conv * (1.0 + lax.erf(conv * (2.0 ** -0.5)))
        hbuf[:, sl] = (gelu * u).astype(BF16)
    for rows in groups:
        ffn = _dot(hbuf[rows, :], wd_ref[...])
        o_ref[0, rows, :] = _layer_norm(alpha * x1buf[rows, :] + ffn, l2g_ref[...], l2b_ref[...])


def _mix_ffn(alpha, x, y_attn, pm, gates, w_a, w_p, w_o, ln1_g, ln1_b,
             w_g, w_u, conv_w, conv_b, w_d, ln2_g, ln2_b):
    b, s, d = x.shape
    tm = ROW_TILE
    row = lambda n: pl.BlockSpec((1, tm, n), lambda bi, i: (bi, i, 0))
    consts = [w_a, w_p, w_o, ln1_g, ln1_b, w_g, w_u, conv_w, conv_b, w_d, ln2_g, ln2_b]
    return pl.pallas_call(
        functools.partial(_mix_ffn_kernel, alpha),
        grid=(b, s // tm),
        in_specs=[row(d), row(D_ATTN), row(D_POOL), row(2 * D_MODEL)]
                 + [_const_spec(c.shape) for c in consts],
        out_specs=row(d),
        out_shape=jax.ShapeDtypeStruct((b, s, d), F32),
        scratch_shapes=[pltpu.VMEM((tm, d), F32),
                        pltpu.VMEM((SUBLANES + tm, D_FF), F32),
                        pltpu.VMEM((tm, D_FF), BF16)],
        compiler_params=pltpu.CompilerParams(
            dimension_semantics=("arbitrary", "arbitrary"), vmem_limit_bytes=VMEM_LIMIT),
        name="mix_ffn",
    )(x, y_attn, pm, gates, *consts)


def _rope_tables(s):
    half = HEAD_DIM // 2
    inv_freq = 1.0 / (ROPE_THETA ** (jnp.arange(half, dtype=F32) / half))
    ang = jnp.arange(s, dtype=F32)[:, None] * inv_freq[None, :]
    cos, sin = jnp.cos(ang), jnp.sin(ang)
    cos = jnp.tile(jnp.concatenate([cos, cos], axis=-1), (1, HEADS_PER_VREG))
    sin = jnp.tile(jnp.concatenate([-sin, sin], axis=-1), (1, HEADS_PER_VREG))
    return cos, sin


def kernel(x, w_in, b_gate, w_branch_attn, w_pool, pool_scale, w_branch_pool, w_out, ln1_g, ln1_b,
           w_ffn_gate, w_ffn_up, conv_w, conv_b, w_ffn_down, ln2_g, ln2_b):
    depth = w_in.shape[0]
    s = x.shape[1]
    alpha = (2.0 * depth) ** 0.25
    cos, sin = _rope_tables(s)
    vec = lambda t: t.reshape(1, -1)
    for l in range(depth):
        q, k, v, pm, gates = _inproj(x, l, w_in, vec(b_gate[l]), cos, sin,
                                     w_pool[l].astype(BF16), vec(pool_scale[l]))
        y_attn, (w_a, w_p, w_o, w_g, w_u, w_d) = _moba(
            q, k, v, l, [w_branch_attn, w_branch_pool, w_out, w_ffn_gate, w_ffn_up, w_ffn_down])
        x = _mix_ffn(alpha, x, y_attn, pm, gates, w_a, w_p, w_o, vec(ln1_g[l]), vec(ln1_b[l]),
                     w_g, w_u, conv_w[l], vec(conv_b[l]), w_d, vec(ln2_g[l]), vec(ln2_b[l]))
    return x
```

```python
import functools

import jax
import jax.numpy as jnp
from jax import lax
from jax.experimental import pallas as pl
from jax.experimental.pallas import tpu as pltpu

D_MODEL = 1024
ATTN_HEADS = 8
HEAD_DIM = 64
D_ATTN = ATTN_HEADS * HEAD_DIM
MOBA_BLOCK = 256
MOBA_TOPK = 3
ROPE_THETA = 10000.0
POOL_WINDOWS = (2, 4, 8, 16)
POOL_GROUP_DIM = 128
D_POOL = len(POOL_WINDOWS) * POOL_GROUP_DIM
D_FF = 2816
CONV_WIDTH = 3
LN_EPS = 1e-5
NEG = -1e30
Q_SCALE = HEAD_DIM ** -0.5 * 1.4426950408889634

LANES = 128
SUBLANES = 8
HEADS_PER_VREG = LANES // HEAD_DIM
BF16_ROWS = 2 * SUBLANES
PV_ROWS = HEAD_DIM + BF16_ROWS
SCORE_BUFFERS = 3
SCORE_SPLIT = 2
POOL_HALO = 16
ROW_TILE = 512
FF_CHUNK = 256
LN_ROWS = 256
VMEM_LIMIT = 52 * 1024 * 1024

F32 = jnp.float32
BF16 = jnp.bfloat16


def _dot(a, b):
    return jnp.dot(a, b, preferred_element_type=F32)


def _const_spec(shape):
    return pl.BlockSpec(shape, lambda *_: (0,) * len(shape), pipeline_mode=pl.Buffered(1))


def _layer_norm(y, g, b):
    mu = jnp.mean(y, axis=-1, keepdims=True)
    d = y - mu
    var = jnp.mean(d * d, axis=-1, keepdims=True)
    return d * lax.rsqrt(var + LN_EPS) * g + b


def _inproj_kernel(x_ref, wf_ref, bg_ref, cos_ref, sin_ref, wpool_ref, pscale_ref,
                   q_ref, k_ref, v_ref, pm_ref, g_ref, w_ref, ubuf, *lvl):
    i = pl.program_id(1)
    tm = x_ref.shape[1]

    @pl.when((pl.program_id(0) == 0) & (i == 0))
    def _():
        for c in range(0, w_ref.shape[1], D_ATTN):
            w_ref[:, c:c + D_ATTN] = wf_ref[0, :, c:c + D_ATTN].astype(BF16)

    xb = x_ref[0].astype(BF16)

    lane = lax.broadcasted_iota(jnp.int32, (tm, LANES), 1)
    first_half = (lane % HEAD_DIM) < (HEAD_DIM // 2)
    cos = cos_ref[...]
    sin = sin_ref[...]

    def rope(t):
        partner = jnp.where(first_half, pltpu.roll(t, LANES - HEAD_DIM // 2, 1),
                            pltpu.roll(t, HEAD_DIM // 2, 1))
        return t * cos + partner * sin

    pad, cur, end = SUBLANES, SUBLANES + POOL_HALO, SUBLANES + POOL_HALO + tm

    @pl.when(i == 0)
    def _():
        ubuf[0:cur, :] = jnp.zeros((cur, D_POOL), F32)
        for level in lvl:
            level[0:pad, :] = jnp.zeros((pad, level.shape[1]), F32)

    o3 = 3 * D_ATTN
    ubuf[cur:end, :] = _dot(xb, w_ref[:, o3:o3 + D_POOL])

    zq = _dot(xb, w_ref[:, 0:D_ATTN])
    for c in range(D_ATTN // LANES):
        sl = slice(c * LANES, (c + 1) * LANES)
        q_ref[0, :, sl] = (rope(zq[:, sl]) * Q_SCALE).astype(BF16)
    zk = _dot(xb, w_ref[:, D_ATTN:2 * D_ATTN])
    for c in range(D_ATTN // LANES):
        sl = slice(c * LANES, (c + 1) * LANES)
        k_ref[0, :, sl] = rope(zk[:, sl]).astype(BF16)
    v_ref[0] = _dot(xb, w_ref[:, 2 * D_ATTN:3 * D_ATTN]).astype(BF16)

    o4 = o3 + D_POOL
    for c in range(2 * D_MODEL // D_ATTN):
        sl = slice(c * D_ATTN, (c + 1) * D_ATTN)
        zg = _dot(xb, w_ref[:, o4 + c * D_ATTN:o4 + (c + 1) * D_ATTN]) + bg_ref[:, sl]
        g_ref[0, :, sl] = jax.nn.sigmoid(zg).astype(BF16)

    pos = i * tm + lax.broadcasted_iota(jnp.int32, (tm, 1), 0)
    prev = ubuf
    for g, w in enumerate(POOL_WINDOWS):
        sl = slice(g * POOL_GROUP_DIM, (g + 1) * POOL_GROUP_DIM)
        sums = prev[pad:end, :] + prev[pad - w // 2:end - w // 2, :]
        if g + 1 < len(POOL_WINDOWS):
            lvl[g][pad:end, :] = sums[:, POOL_GROUP_DIM:]
            prev = lvl[g]
        inv_count = 1.0 / jnp.minimum(pos + 1, w).astype(F32)
        pooled = sums[POOL_HALO:, 0:POOL_GROUP_DIM] * inv_count - ubuf[cur:end, sl]
        mixed = _dot(pooled.astype(BF16), wpool_ref[g]) * pscale_ref[:, sl]
        pm_ref[0, :, sl] = mixed.astype(BF16)
    ubuf[pad:cur, :] = ubuf[end - POOL_HALO:end, :]


def _inproj(x, layer, w_in, b_gate, cos, sin, w_pool, pool_scale):
    b, s, d = x.shape
    tm = ROW_TILE
    n_in = w_in.shape[2]
    assert all(w == 2 ** (g + 1) for g, w in enumerate(POOL_WINDOWS)) and POOL_WINDOWS[-1] <= POOL_HALO
    row = lambda n: pl.BlockSpec((1, tm, n), lambda bi, i: (bi, i, 0))
    return pl.pallas_call(
        _inproj_kernel,
        grid=(b, s // tm),
        in_specs=[
            row(d),
            pl.BlockSpec((1, d, n_in), lambda *_: (layer, 0, 0), pipeline_mode=pl.Buffered(1)),
            _const_spec((1, 2 * D_MODEL)),
            pl.BlockSpec((tm, LANES), lambda bi, i: (i, 0)),
            pl.BlockSpec((tm, LANES), lambda bi, i: (i, 0)),
            _const_spec(w_pool.shape),
            _const_spec((1, D_POOL)),
        ],
        out_specs=[row(D_ATTN), row(D_ATTN), row(D_ATTN), row(D_POOL), row(2 * D_MODEL)],
        out_shape=[
            jax.ShapeDtypeStruct((b, s, D_ATTN), BF16),
            jax.ShapeDtypeStruct((b, s, D_ATTN), BF16),
            jax.ShapeDtypeStruct((b, s, D_ATTN), BF16),
            jax.ShapeDtypeStruct((b, s, D_POOL), BF16),
            jax.ShapeDtypeStruct((b, s, 2 * D_MODEL), BF16),
        ],
        scratch_shapes=[pltpu.VMEM((d, n_in), BF16)]
                       + [pltpu.VMEM((SUBLANES + POOL_HALO + tm, D_POOL - g * POOL_GROUP_DIM), F32)
                          for g in range(len(POOL_WINDOWS))],
        compiler_params=pltpu.CompilerParams(
            dimension_semantics=("arbitrary", "arbitrary"), vmem_limit_bytes=VMEM_LIMIT),
        name="inproj",
    )(x, w_in, b_gate, cos, sin, w_pool, pool_scale)


def _moba_kernel(n_cast, q_ref, k_ref, v_ref, *refs):
    w_refs, o_ref, wb_refs = refs[:n_cast], refs[n_cast], refs[n_cast + 1:2 * n_cast + 1]
    khat_ref, vt_ref, qt_ref, *s_refs = refs[2 * n_cast + 1:]
    for w_ref, wb_ref in zip(w_refs, wb_refs):
        wb_ref[...] = w_ref[0].astype(BF16)

    seq = k_ref.shape[1]
    nb = seq // MOBA_BLOCK
    blk = lambda j: slice(j * MOBA_BLOCK, (j + 1) * MOBA_BLOCK)
    lane = lax.broadcasted_iota(jnp.int32, (MOBA_BLOCK, LANES), 1)

    kf = k_ref[0].astype(F32).reshape(nb, MOBA_BLOCK, LANES)
    km = jnp.sum(kf, axis=1) * (1.0 / MOBA_BLOCK)
    hi = km.astype(BF16)
    kmb = jnp.concatenate([hi, (km - hi.astype(F32)).astype(BF16)], axis=0)

    for j in range(nb):
        kj = k_ref[0, blk(j), :]
        for h in range(HEADS_PER_VREG):
            onehot = jnp.where(lane == (1 - h) * HEAD_DIM + j, 1.0, 0.0).astype(BF16)
            khat_ref[h, blk(j), :] = jnp.where((lane // HEAD_DIM) == h, kj, onehot)
        vt = v_ref[0, blk(j), :].astype(F32).T.astype(BF16)
        for h in range(HEADS_PER_VREG):
            vt_ref[h, 0:HEAD_DIM, blk(j)] = vt[h * HEAD_DIM:(h + 1) * HEAD_DIM]
            vt_ref[h, HEAD_DIM:PV_ROWS, blk(j)] = jnp.ones((PV_ROWS - HEAD_DIM, MOBA_BLOCK), BF16)
        qt_ref[:, blk(j)] = q_ref[0, blk(j), :].astype(F32).T.astype(BF16)

    dim_i = lax.broadcasted_iota(jnp.int32, (LANES, MOBA_BLOCK), 0)
    key_i = lax.broadcasted_iota(jnp.int32, (MOBA_BLOCK, MOBA_BLOCK), 0)
    qry_i = lax.broadcasted_iota(jnp.int32, (MOBA_BLOCK, MOBA_BLOCK), 1)
    causal = key_i <= qry_i
    blk_i = lax.broadcasted_iota(jnp.int32, (nb, MOBA_BLOCK), 0)
    fold = lambda t: t.reshape(MOBA_BLOCK // SUBLANES, SUBLANES, MOBA_BLOCK)

    units = [(i, h) for i in range(nb) for h in range(HEADS_PER_VREG)]
    state = {}
    sbuf = lambda u: s_refs[units.index(u) % len(s_refs)]

    def prepare(u):
        i, h = u
        in_head = (dim_i // HEAD_DIM) == h
        q = qt_ref[:, blk(i)]
        if i > MOBA_TOPK:
            g2 = _dot(kmb, jnp.where(in_head, q, jnp.zeros_like(q)))
            gate = g2[0:nb] + g2[nb:2 * nb]
            beaten = jnp.zeros((nb, MOBA_BLOCK), jnp.int32)
            for m in range(i):
                gm = gate[m:m + 1, :]
                wins = (gm > gate) | ((gm == gate) & (m < blk_i))
                beaten = beaten + jnp.where(wins, 1, 0)
            bias = jnp.where((blk_i < i) & (beaten >= MOBA_TOPK), NEG, 0.0)
            base = (1 - h) * HEAD_DIM
            parts = [bias, jnp.zeros((LANES - base - nb, MOBA_BLOCK), F32)]
            if base:
                parts = [jnp.zeros((base, MOBA_BLOCK), F32)] + parts
            spare = jnp.concatenate(parts, axis=0).astype(BF16)
        else:
            spare = jnp.zeros_like(q)
        state[u] = dict(q_aug=jnp.where(in_head, q, spare),
                        m_run=jnp.full((SUBLANES, MOBA_BLOCK), NEG, F32),
                        acc=jnp.zeros((PV_ROWS, MOBA_BLOCK), F32))

    def score_matmul(u):
        q_aug = state[u].pop("q_aug")
        n = u[0] + 1
        for j in range(n):
            sj = _dot(khat_ref[u[1], blk(j), :], q_aug)
            sbuf(u)[blk(j), :] = sj
            if j == u[0]:
                sj = jnp.where(causal, sj, NEG)
            state[u]["m_run"] = jnp.maximum(state[u]["m_run"], jnp.max(fold(sj), axis=0))
        end_scores(u)

    def masked_scores(u, j):
        s = sbuf(u)[blk(j), :]
        return jnp.where(causal, s, NEG) if j == u[0] else s

    def score_step(u, j):
        st = state[u]
        st["m_run"] = jnp.maximum(st["m_run"], jnp.max(fold(masked_scores(u, j)), axis=0))

    def end_scores(u):
        state[u]["m_fin"] = jnp.max(state[u]["m_run"], axis=0, keepdims=True)

    def value_step(u, j):
        st = state[u]
        p = jnp.exp2(masked_scores(u, j) - st["m_fin"])
        st["acc"] = st["acc"] + _dot(vt_ref[u[1], :, blk(j)], p.astype(BF16))

    def finish(u):
        i, h = u
        acc = state.pop(u)["acc"]
        out = acc[0:HEAD_DIM] / acc[HEAD_DIM:HEAD_DIM + 1]
        if h == 0:
            state["head0"] = out
        else:
            both = jnp.concatenate([state.pop("head0"), out], axis=0)
            o_ref[0, blk(i), :] = both.T.astype(BF16)

    def stage_tasks(t):
        unit = lambda d: units[t + d] if 0 <= t + d < len(units) else None
        stages = []
        if unit(3):
            stages.append([functools.partial(prepare, unit(3))])
        if unit(2):
            stages.append([functools.partial(score_matmul, unit(2))])
        if unit(0):
            stages.append([functools.partial(value_step, unit(0), j) for j in range(unit(0)[0] + 1)])
        return stages

    for t in range(-3, len(units)):
        stages = stage_tasks(t)
        n = max(len(tasks) for tasks in stages)
        for step in range(n):
            for tasks in stages:
                lo, hi_ = step * len(tasks) // n, (step + 1) * len(tasks) // n
                for task in tasks[lo:hi_]:
                    task()
        if t >= 0:
            finish(units[t])


def _cast_rows(rows, steps):
    r = -(-rows // steps)
    r = -(-r // BF16_ROWS) * BF16_ROWS
    while rows % r:
        r += BF16_ROWS
    return r


def _moba(q, k, v, layer, weights):
    b, s, _ = q.shape
    n_pairs = D_ATTN // LANES
    full = pl.BlockSpec((1, s, LANES), lambda bi, p: (bi, 0, p))
    w_in_specs, w_out_specs, w_out_shapes = [], [], []
    for w in weights:
        _, rows, cols = w.shape
        r = _cast_rows(rows, b * n_pairs)
        last = rows // r - 1
        w_in_specs.append(pl.BlockSpec(
            (1, r, cols), lambda bi, p, last=last: (layer, jnp.minimum(bi * n_pairs + p, last), 0)))
        w_out_specs.append(pl.BlockSpec(
            (r, cols), lambda bi, p, last=last: (jnp.minimum(bi * n_pairs + p, last), 0)))
        w_out_shapes.append(jax.ShapeDtypeStruct((rows, cols), BF16))
    y_attn, *w_bf16 = pl.pallas_call(
        functools.partial(_moba_kernel, len(weights)),
        grid=(b, n_pairs),
        in_specs=[full, full, full] + w_in_specs,
        out_specs=[full] + w_out_specs,
        out_shape=[jax.ShapeDtypeStruct((b, s, D_ATTN), BF16)] + w_out_shapes,
        scratch_shapes=[pltpu.VMEM((HEADS_PER_VREG, s, LANES), BF16),
                        pltpu.VMEM((HEADS_PER_VREG, PV_ROWS, s), BF16),
                        pltpu.VMEM((LANES, s), BF16),
                        *[pltpu.VMEM((s, MOBA_BLOCK), F32)] * SCORE_BUFFERS],
        compiler_params=pltpu.CompilerParams(
            dimension_semantics=("arbitrary", "arbitrary"), vmem_limit_bytes=VMEM_LIMIT),
        name="moba",
    )(q, k, v, *weights)
    return y_attn, w_bf16


def _mix_ffn_kernel(alpha, x_ref, ya_ref, pm_ref, g_ref, wa_ref, wp_ref, wo_ref, l1g_ref, l1b_ref,
                    wg_ref, wu_ref, cw_ref, cb_ref, wd_ref, l2g_ref, l2b_ref, o_ref,
                    x1buf, abuf, hbuf):
    i = pl.program_id(1)
    tm = x_ref.shape[1]
    halo = SUBLANES
    groups = [slice(r * LN_ROWS, (r + 1) * LN_ROWS) for r in range(tm // LN_ROWS)]

    @pl.when(i == 0)
    def _():
        abuf[0:halo, :] = jnp.zeros((halo, D_FF), F32)

    for rows in groups:
        y_attn = _dot(ya_ref[0, rows, :], wa_ref[...])
        y_pool = _dot(pm_ref[0, rows, :], wp_ref[...])
        g_attn = g_ref[0, rows, 0:D_MODEL].astype(F32)
        g_pool = g_ref[0, rows, D_MODEL:2 * D_MODEL].astype(F32)
        mix = _dot((g_attn * y_attn + g_pool * y_pool).astype(BF16), wo_ref[...])
        x1buf[rows, :] = _layer_norm(alpha * x_ref[0, rows, :] + mix, l1g_ref[...], l1b_ref[...])

    xb = x1buf[...].astype(BF16)
    for c in range(D_FF // FF_CHUNK):
        sl = slice(c * FF_CHUNK, (c + 1) * FF_CHUNK)
        if c == 0:
            parts = [(_dot(xg, wg_ref[:, sl]), _dot(xg, wu_ref[:, sl]))
                     for xg in (x1buf[rows, :].astype(BF16) for rows in groups)]
            a = jnp.concatenate([p[0] for p in parts])
            u = jnp.concatenate([p[1] for p in parts])
        else:
            a = _dot(xb, wg_ref[:, sl])
            u = _dot(xb, wu_ref[:, sl])
        abuf[halo:halo + tm, sl] = a
        conv = a * cw_ref[CONV_WIDTH - 1:CONV_WIDTH, sl] + cb_ref[:, sl]
        for t in range(CONV_WIDTH - 1):
            back = CONV_WIDTH - 1 - t
            conv = conv + abuf[halo - back:halo - back + tm, sl] * cw_ref[t:t + 1, sl]
        abuf[0:halo, sl] = a[tm - halo:tm, :]
        gelu = 0.5 * conv * (1.0 + lax.erf(conv * (2.0 ** -0.5)))
        hbuf[:, sl] = (gelu * u).astype(BF16)
    for rows in groups:
        ffn = _dot(hbuf[rows, :], wd_ref[...])
        o_ref[0, rows, :] = _layer_norm(alpha * x1buf[rows, :] + ffn, l2g_ref[...], l2b_ref[...])


def _mix_ffn(alpha, x, y_attn, pm, gates, w_a, w_p, w_o, ln1_g, ln1_b,
             w_g, w_u, conv_w, conv_b, w_d, ln2_g, ln2_b):
    b, s, d = x.shape
    tm = ROW_TILE
    row = lambda n: pl.BlockSpec((1, tm, n), lambda bi, i: (bi, i, 0))
    consts = [w_a, w_p, w_o, ln1_g, ln1_b, w_g, w_u, conv_w, conv_b, w_d, ln2_g, ln2_b]
    return pl.pallas_call(
        functools.partial(_mix_ffn_kernel, alpha),
        grid=(b, s // tm),
        in_specs=[row(d), row(D_ATTN), row(D_POOL), row(2 * D_MODEL)]
                 + [_const_spec(c.shape) for c in consts],
        out_specs=row(d),
        out_shape=jax.ShapeDtypeStruct((b, s, d), F32),
        scratch_shapes=[pltpu.VMEM((tm, d), F32),
                        pltpu.VMEM((SUBLANES + tm, D_FF), F32),
                        pltpu.VMEM((tm, D_FF), BF16)],
        compiler_params=pltpu.CompilerParams(
            dimension_semantics=("arbitrary", "arbitrary"), vmem_limit_bytes=VMEM_LIMIT),
        name="mix_ffn",
    )(x, y_attn, pm, gates, *consts)


def _rope_tables(s):
    half = HEAD_DIM // 2
    inv_freq = 1.0 / (ROPE_THETA ** (jnp.arange(half, dtype=F32) / half))
    ang = jnp.arange(s, dtype=F32)[:, None] * inv_freq[None, :]
    cos, sin = jnp.cos(ang), jnp.sin(ang)
    cos = jnp.tile(jnp.concatenate([cos, cos], axis=-1), (1, HEADS_PER_VREG))
    sin = jnp.tile(jnp.concatenate([-sin, sin], axis=-1), (1, HEADS_PER_VREG))
    return cos, sin


def kernel(x, w_in, b_gate, w_branch_attn, w_pool, pool_scale, w_branch_pool, w_out, ln1_g, ln1_b,
           w_ffn_gate, w_ffn_up, conv_w, conv_b, w_ffn_down, ln2_g, ln2_b):
    depth = w_in.shape[0]
    s = x.shape[1]
    alpha = (2.0 * depth) ** 0.25
    cos, sin = _rope_tables(s)
    vec = lambda t: t.reshape(1, -1)
    for l in range(depth):
        q, k, v, pm, gates = _inproj(x, l, w_in, vec(b_gate[l]), cos, sin,
                                     w_pool[l].astype(BF16), vec(pool_scale[l]))
        y_attn, (w_a, w_p, w_o, w_g, w_u, w_d) = _moba(
            q, k, v, l, [w_branch_attn, w_branch_pool, w_out, w_ffn_gate, w_ffn_up, w_ffn_down])
        x = _mix_ffn(alpha, x, y_attn, pm, gates, w_a, w_p, w_o, vec(ln1_g[l]), vec(ln1_b[l]),
                     w_g, w_u, conv_w[l], vec(conv_b[l]), w_d, vec(ln2_g[l]), vec(ln2_b[l]))
    return x
```

```python
import functools

import jax
import jax.numpy as jnp
from jax import lax
from jax.experimental import pallas as pl
from jax.experimental.pallas import tpu as pltpu

D_MODEL = 1024
ATTN_HEADS = 8
HEAD_DIM = 64
D_ATTN = ATTN_HEADS * HEAD_DIM
MOBA_BLOCK = 256
MOBA_TOPK = 3
ROPE_THETA = 10000.0
POOL_WINDOWS = (2, 4, 8, 16)
POOL_GROUP_DIM = 128
D_POOL = len(POOL_WINDOWS) * POOL_GROUP_DIM
D_FF = 2816
CONV_WIDTH = 3
LN_EPS = 1e-5
NEG = -1e30
Q_SCALE = HEAD_DIM ** -0.5 * 1.4426950408889634

LANES = 128
SUBLANES = 8
HEADS_PER_VREG = LANES // HEAD_DIM
BF16_ROWS = 2 * SUBLANES
PV_ROWS = HEAD_DIM + BF16_ROWS
SCORE_BUFFERS = 3
SCORE_SPLIT = 2
PAIRS_PER_STEP = 2
POOL_HALO = 16
ROW_TILE = 512
FF_CHUNK = 256
LN_ROWS = 256
VMEM_LIMIT = 52 * 1024 * 1024

F32 = jnp.float32
BF16 = jnp.bfloat16


def _dot(a, b):
    return jnp.dot(a, b, preferred_element_type=F32)


def _const_spec(shape):
    return pl.BlockSpec(shape, lambda *_: (0,) * len(shape), pipeline_mode=pl.Buffered(1))


def _layer_norm(y, g, b):
    mu = jnp.mean(y, axis=-1, keepdims=True)
    d = y - mu
    var = jnp.mean(d * d, axis=-1, keepdims=True)
    return d * lax.rsqrt(var + LN_EPS) * g + b


def _inproj_kernel(x_ref, wf_ref, bg_ref, cos_ref, sin_ref, wpool_ref, pscale_ref,
                   q_ref, k_ref, v_ref, pm_ref, g_ref, w_ref, ubuf, *lvl):
    i = pl.program_id(1)
    tm = x_ref.shape[1]

    @pl.when((pl.program_id(0) == 0) & (i == 0))
    def _():
        for c in range(0, w_ref.shape[1], D_ATTN):
            w_ref[:, c:c + D_ATTN] = wf_ref[0, :, c:c + D_ATTN].astype(BF16)

    xb = x_ref[0].astype(BF16)

    lane = lax.broadcasted_iota(jnp.int32, (tm, LANES), 1)
    first_half = (lane % HEAD_DIM) < (HEAD_DIM // 2)
    cos = cos_ref[...]
    sin = sin_ref[...]

    def rope(t):
        partner = jnp.where(first_half, pltpu.roll(t, LANES - HEAD_DIM // 2, 1),
                            pltpu.roll(t, HEAD_DIM // 2, 1))
        return t * cos + partner * sin

    pad, cur, end = SUBLANES, SUBLANES + POOL_HALO, SUBLANES + POOL_HALO + tm

    @pl.when(i == 0)
    def _():
        ubuf[0:cur, :] = jnp.zeros((cur, D_POOL), F32)
        for level in lvl:
            level[0:pad, :] = jnp.zeros((pad, level.shape[1]), F32)

    o3 = 3 * D_ATTN
    ubuf[cur:end, :] = _dot(xb, w_ref[:, o3:o3 + D_POOL])

    zq = _dot(xb, w_ref[:, 0:D_ATTN])
    for c in range(D_ATTN // LANES):
        sl = slice(c * LANES, (c + 1) * LANES)
        q_ref[0, :, sl] = (rope(zq[:, sl]) * Q_SCALE).astype(BF16)
    zk = _dot(xb, w_ref[:, D_ATTN:2 * D_ATTN])
    for c in range(D_ATTN // LANES):
        sl = slice(c * LANES, (c + 1) * LANES)
        k_ref[0, :, sl] = rope(zk[:, sl]).astype(BF16)
    v_ref[0] = _dot(xb, w_ref[:, 2 * D_ATTN:3 * D_ATTN]).astype(BF16)

    o4 = o3 + D_POOL
    for c in range(2 * D_MODEL // D_ATTN):
        sl = slice(c * D_ATTN, (c + 1) * D_ATTN)
        zg = _dot(xb, w_ref[:, o4 + c * D_ATTN:o4 + (c + 1) * D_ATTN]) + bg_ref[:, sl]
        g_ref[0, :, sl] = jax.nn.sigmoid(zg).astype(BF16)

    pos = i * tm + lax.broadcasted_iota(jnp.int32, (tm, 1), 0)
    prev = ubuf
    for g, w in enumerate(POOL_WINDOWS):
        sl = slice(g * POOL_GROUP_DIM, (g + 1) * POOL_GROUP_DIM)
        sums = prev[pad:end, :] + prev[pad - w // 2:end - w // 2, :]
        if g + 1 < len(POOL_WINDOWS):
            lvl[g][pad:end, :] = sums[:, POOL_GROUP_DIM:]
            prev = lvl[g]
        inv_count = 1.0 / jnp.minimum(pos + 1, w).astype(F32)
        pooled = sums[POOL_HALO:, 0:POOL_GROUP_DIM] * inv_count - ubuf[cur:end, sl]
        mixed = _dot(pooled.astype(BF16), wpool_ref[g]) * pscale_ref[:, sl]
        pm_ref[0, :, sl] = mixed.astype(BF16)
    ubuf[pad:cur, :] = ubuf[end - POOL_HALO:end, :]


def _inproj(x, layer, w_in, b_gate, cos, sin, w_pool, pool_scale):
    b, s, d = x.shape
    tm = ROW_TILE
    n_in = w_in.shape[2]
    assert all(w == 2 ** (g + 1) for g, w in enumerate(POOL_WINDOWS)) and POOL_WINDOWS[-1] <= POOL_HALO
    row = lambda n: pl.BlockSpec((1, tm, n), lambda bi, i: (bi, i, 0))
    return pl.pallas_call(
        _inproj_kernel,
        grid=(b, s // tm),
        in_specs=[
            row(d),
            pl.BlockSpec((1, d, n_in), lambda *_: (layer, 0, 0), pipeline_mode=pl.Buffered(1)),
            _const_spec((1, 2 * D_MODEL)),
            pl.BlockSpec((tm, LANES), lambda bi, i: (i, 0)),
            pl.BlockSpec((tm, LANES), lambda bi, i: (i, 0)),
            _const_spec(w_pool.shape),
            _const_spec((1, D_POOL)),
        ],
        out_specs=[row(D_ATTN), row(D_ATTN), row(D_ATTN), row(D_POOL), row(2 * D_MODEL)],
        out_shape=[
            jax.ShapeDtypeStruct((b, s, D_ATTN), BF16),
            jax.ShapeDtypeStruct((b, s, D_ATTN), BF16),
            jax.ShapeDtypeStruct((b, s, D_ATTN), BF16),
            jax.ShapeDtypeStruct((b, s, D_POOL), BF16),
            jax.ShapeDtypeStruct((b, s, 2 * D_MODEL), BF16),
        ],
        scratch_shapes=[pltpu.VMEM((d, n_in), BF16)]
                       + [pltpu.VMEM((SUBLANES + POOL_HALO + tm, D_POOL - g * POOL_GROUP_DIM), F32)
                          for g in range(len(POOL_WINDOWS))],
        compiler_params=pltpu.CompilerParams(
            dimension_semantics=("arbitrary", "arbitrary"), vmem_limit_bytes=VMEM_LIMIT),
        name="inproj",
    )(x, w_in, b_gate, cos, sin, w_pool, pool_scale)


def _moba_kernel(n_cast, q_ref, k_ref, v_ref, *refs):
    w_refs, o_ref, wb_refs = refs[:n_cast], refs[n_cast], refs[n_cast + 1:2 * n_cast + 1]
    khat_ref, vt_ref, qt_ref, *s_refs = refs[2 * n_cast + 1:]
    for w_ref, wb_ref in zip(w_refs, wb_refs):
        wb_ref[...] = w_ref[0].astype(BF16)

    seq = k_ref.shape[1]
    nb = seq // MOBA_BLOCK
    blk = lambda j: slice(j * MOBA_BLOCK, (j + 1) * MOBA_BLOCK)
    lane = lax.broadcasted_iota(jnp.int32, (MOBA_BLOCK, LANES), 1)
    n_pairs = q_ref.shape[2] // LANES
    pair = lambda p: slice(p * LANES, (p + 1) * LANES)

    kmbs = []
    for p in range(n_pairs):
        kf = k_ref[0, :, pair(p)].astype(F32).reshape(nb, MOBA_BLOCK, LANES)
        km = jnp.sum(kf, axis=1) * (1.0 / MOBA_BLOCK)
        hi = km.astype(BF16)
        kmbs.append(jnp.concatenate([hi, (km - hi.astype(F32)).astype(BF16)], axis=0))

        for j in range(nb):
            kj = k_ref[0, blk(j), pair(p)]
            for h in range(HEADS_PER_VREG):
                g = p * HEADS_PER_VREG + h
                onehot = jnp.where(lane == (1 - h) * HEAD_DIM + j, 1.0, 0.0).astype(BF16)
                khat_ref[g, blk(j), :] = jnp.where((lane // HEAD_DIM) == h, kj, onehot)
            vt = v_ref[0, blk(j), pair(p)].astype(F32).T.astype(BF16)
            for h in range(HEADS_PER_VREG):
                g = p * HEADS_PER_VREG + h
                vt_ref[g, 0:HEAD_DIM, blk(j)] = vt[h * HEAD_DIM:(h + 1) * HEAD_DIM]
                vt_ref[g, HEAD_DIM:PV_ROWS, blk(j)] = jnp.ones((PV_ROWS - HEAD_DIM, MOBA_BLOCK), BF16)
            qt_ref[p, :, blk(j)] = q_ref[0, blk(j), pair(p)].astype(F32).T.astype(BF16)

    dim_i = lax.broadcasted_iota(jnp.int32, (LANES, MOBA_BLOCK), 0)
    key_i = lax.broadcasted_iota(jnp.int32, (MOBA_BLOCK, MOBA_BLOCK), 0)
    qry_i = lax.broadcasted_iota(jnp.int32, (MOBA_BLOCK, MOBA_BLOCK), 1)
    causal = key_i <= qry_i
    blk_i = lax.broadcasted_iota(jnp.int32, (nb, MOBA_BLOCK), 0)
    fold = lambda t: t.reshape(MOBA_BLOCK // SUBLANES, SUBLANES, MOBA_BLOCK)

    units = [(i, p * HEADS_PER_VREG + h)
             for p in range(n_pairs) for i in range(nb) for h in range(HEADS_PER_VREG)]
    state = {}
    sbuf = lambda u: s_refs[units.index(u) % len(s_refs)]

    def prepare(u):
        i, (p, h) = u[0], divmod(u[1], HEADS_PER_VREG)
        in_head = (dim_i // HEAD_DIM) == h
        q = qt_ref[p, :, blk(i)]
        if i > MOBA_TOPK:
            g2 = _dot(kmbs[p], jnp.where(in_head, q, jnp.zeros_like(q)))
            gate = g2[0:nb] + g2[nb:2 * nb]
            beaten = jnp.zeros((nb, MOBA_BLOCK), jnp.int32)
            for m in range(i):
                gm = gate[m:m + 1, :]
                wins = (gm > gate) | ((gm == gate) & (m < blk_i))
                beaten = beaten + jnp.where(wins, 1, 0)
            bias = jnp.where((blk_i < i) & (beaten >= MOBA_TOPK), NEG, 0.0)
            base = (1 - h) * HEAD_DIM
            parts = [bias, jnp.zeros((LANES - base - nb, MOBA_BLOCK), F32)]
            if base:
                parts = [jnp.zeros((base, MOBA_BLOCK), F32)] + parts
            spare = jnp.concatenate(parts, axis=0).astype(BF16)
        else:
            spare = jnp.zeros_like(q)
        state[u] = dict(q_aug=jnp.where(in_head, q, spare),
                        m_run=jnp.full((SUBLANES, MOBA_BLOCK), NEG, F32),
                        acc=jnp.zeros((PV_ROWS, MOBA_BLOCK), F32))

    def score_matmul(u):
        q_aug = state[u].pop("q_aug")
        n = u[0] + 1
        for lo in range(0, n, SCORE_SPLIT):
            keys = slice(lo * MOBA_BLOCK, min(lo + SCORE_SPLIT, n) * MOBA_BLOCK)
            sbuf(u)[keys, :] = _dot(khat_ref[u[1], keys, :], q_aug)

    def masked_scores(u, j):
        s = sbuf(u)[blk(j), :]
        return jnp.where(causal, s, NEG) if j == u[0] else s

    def score_step(u, j):
        st = state[u]
        st["m_run"] = jnp.maximum(st["m_run"], jnp.max(fold(masked_scores(u, j)), axis=0))

    def end_scores(u):
        state[u]["m_fin"] = jnp.max(state[u]["m_run"], axis=0, keepdims=True)

    def value_step(u, j):
        st = state[u]
        p = jnp.exp2(masked_scores(u, j) - st["m_fin"])
        st["acc"] = st["acc"] + _dot(vt_ref[u[1], :, blk(j)], p.astype(BF16))

    def finish(u):
        i, (p, h) = u[0], divmod(u[1], HEADS_PER_VREG)
        acc = state.pop(u)["acc"]
        out = acc[0:HEAD_DIM] / acc[HEAD_DIM:HEAD_DIM + 1]
        if h == 0:
            state["head0"] = out
        else:
            both = jnp.concatenate([state.pop("head0"), out], axis=0)
            o_ref[0, blk(i), pair(p)] = both.T.astype(BF16)

    def stage_tasks(t):
        unit = lambda d: units[t + d] if 0 <= t + d < len(units) else None
        stages = []
        if unit(3):
            stages.append([functools.partial(prepare, unit(3))])
        if unit(2):
            stages.append([functools.partial(score_matmul, unit(2))])
        if unit(1):
            stages.append([functools.partial(score_step, unit(1), j) for j in range(unit(1)[0] + 1)]
                          + [functools.partial(end_scores, unit(1))])
        if unit(0):
            stages.append([functools.partial(value_step, unit(0), j) for j in range(unit(0)[0] + 1)])
        return stages

    for t in range(-3, len(units)):
        stages = stage_tasks(t)
        n = max(len(tasks) for tasks in stages)
        for step in range(n):
            for tasks in stages:
                lo, hi_ = step * len(tasks) // n, (step + 1) * len(tasks) // n
                for task in tasks[lo:hi_]:
                    task()
        if t >= 0:
            finish(units[t])


def _cast_rows(rows, steps):
    r = -(-rows // steps)
    r = -(-r // BF16_ROWS) * BF16_ROWS
    while rows % r:
        r += BF16_ROWS
    return r


def _moba(q, k, v, layer, weights):
    b, s, _ = q.shape
    width = PAIRS_PER_STEP * LANES
    n_steps = D_ATTN // width
    full = pl.BlockSpec((1, s, width), lambda bi, p: (bi, 0, p))
    w_in_specs, w_out_specs, w_out_shapes = [], [], []
    for w in weights:
        _, rows, cols = w.shape
        r = _cast_rows(rows, b * n_steps)
        last = rows // r - 1
        w_in_specs.append(pl.BlockSpec(
            (1, r, cols), lambda bi, p, last=last: (layer, jnp.minimum(bi * n_steps + p, last), 0)))
        w_out_specs.append(pl.BlockSpec(
            (r, cols), lambda bi, p, last=last: (jnp.minimum(bi * n_steps + p, last), 0)))
        w_out_shapes.append(jax.ShapeDtypeStruct((rows, cols), BF16))
    y_attn, *w_bf16 = pl.pallas_call(
        functools.partial(_moba_kernel, len(weights)),
        grid=(b, n_steps),
        in_specs=[full, full, full] + w_in_specs,
        out_specs=[full] + w_out_specs,
        out_shape=[jax.ShapeDtypeStruct((b, s, D_ATTN), BF16)] + w_out_shapes,
        scratch_shapes=[pltpu.VMEM((PAIRS_PER_STEP * HEADS_PER_VREG, s, LANES), BF16),
                        pltpu.VMEM((PAIRS_PER_STEP * HEADS_PER_VREG, PV_ROWS, s), BF16),
                        pltpu.VMEM((PAIRS_PER_STEP, LANES, s), BF16),
                        *[pltpu.VMEM((s, MOBA_BLOCK), F32)] * SCORE_BUFFERS],
        compiler_params=pltpu.CompilerParams(
            dimension_semantics=("arbitrary", "arbitrary"), vmem_limit_bytes=VMEM_LIMIT),
        name="moba",
    )(q, k, v, *weights)
    return y_attn, w_bf16


def _mix_ffn_kernel(alpha, x_ref, ya_ref, pm_ref, g_ref, wa_ref, wp_ref, wo_ref, l1g_ref, l1b_ref,
                    wg_ref, wu_ref, cw_ref, cb_ref, wd_ref, l2g_ref, l2b_ref, o_ref,
                    x1buf, abuf, hbuf):
    i = pl.program_id(1)
    tm = x_ref.shape[1]
    halo = SUBLANES
    groups = [slice(r * LN_ROWS, (r + 1) * LN_ROWS) for r in range(tm // LN_ROWS)]

    @pl.when(i == 0)
    def _():
        abuf[0:halo, :] = jnp.zeros((halo, D_FF), F32)

    for rows in groups:
        y_attn = _dot(ya_ref[0, rows, :], wa_ref[...])
        y_pool = _dot(pm_ref[0, rows, :], wp_ref[...])
        g_attn = g_ref[0, rows, 0:D_MODEL].astype(F32)
        g_pool = g_ref[0, rows, D_MODEL:2 * D_MODEL].astype(F32)
        mix = _dot((g_attn * y_attn + g_pool * y_pool).astype(BF16), wo_ref[...])
        x1buf[rows, :] = _layer_norm(alpha * x_ref[0, rows, :] + mix, l1g_ref[...], l1b_ref[...])

    xb = x1buf[...].astype(BF16)
    for c in range(D_FF // FF_CHUNK):
        sl = slice(c * FF_CHUNK, (c + 1) * FF_CHUNK)
        if c == 0:
            parts = [(_dot(xg, wg_ref[:, sl]), _dot(xg, wu_ref[:, sl]))
                     for xg in (x1buf[rows, :].astype(BF16) for rows in groups)]
            a = jnp.concatenate([p[0] for p in parts])
            u = jnp.concatenate([p[1] for p in parts])
        else:
            a = _dot(xb, wg_ref[:, sl])
            u = _dot(xb, wu_ref[:, sl])
        abuf[halo:halo + tm, sl] = a
        conv = a * cw_ref[CONV_WIDTH - 1:CONV_WIDTH, sl] + cb_ref[:, sl]
        for t in range(CONV_WIDTH - 1):
            back = CONV_WIDTH - 1 - t
            conv = conv + abuf[halo - back:halo - back + tm, sl] * cw_ref[t:t + 1, sl]
        abuf[0:halo, sl] = a[tm - halo:tm, :]
        gelu = 0.5 * conv * (1.0 + lax.erf(conv * (2.0 ** -0.5)))
        hbuf[:, sl] = (gelu * u).astype(BF16)
    for rows in groups:
        ffn = _dot(hbuf[rows, :], wd_ref[...])
        o_ref[0, rows, :] = _layer_norm(alpha * x1buf[rows, :] + ffn, l2g_ref[...], l2b_ref[...])


def _mix_ffn(alpha, x, y_attn, pm, gates, w_a, w_p, w_o, ln1_g, ln1_b,
             w_g, w_u, conv_w, conv_b, w_d, ln2_g, ln2_b):
    b, s, d = x.shape
    tm = ROW_TILE
    row = lambda n: pl.BlockSpec((1, tm, n), lambda bi, i: (bi, i, 0))
    consts = [w_a, w_p, w_o, ln1_g, ln1_b, w_g, w_u, conv_w, conv_b, w_d, ln2_g, ln2_b]
    return pl.pallas_call(
        functools.partial(_mix_ffn_kernel, alpha),
        grid=(b, s // tm),
        in_specs=[row(d), row(D_ATTN), row(D_POOL), row(2 * D_MODEL)]
                 + [_const_spec(c.shape) for c in consts],
        out_specs=row(d),
        out_shape=jax.ShapeDtypeStruct((b, s, d), F32),
        scratch_shapes=[pltpu.VMEM((tm, d), F32),
                        pltpu.VMEM((SUBLANES + tm, D_FF), F32),
                        pltpu.VMEM((tm, D_FF), BF16)],
        compiler_params=pltpu.CompilerParams(
            dimension_semantics=("arbitrary", "arbitrary"), vmem_limit_bytes=VMEM_LIMIT),
        name="mix_ffn",
    )(x, y_attn, pm, gates, *consts)


def _rope_tables(s):
    half = HEAD_DIM // 2
    inv_freq = 1.0 / (ROPE_THETA ** (jnp.arange(half, dtype=F32) / half))
    ang = jnp.arange(s, dtype=F32)[:, None] * inv_freq[None, :]
    cos, sin = jnp.cos(ang), jnp.sin(ang)
    cos = jnp.tile(jnp.concatenate([cos, cos], axis=-1), (1, HEADS_PER_VREG))
    sin = jnp.tile(jnp.concatenate([-sin, sin], axis=-1), (1, HEADS_PER_VREG))
    return cos, sin


def kernel(x, w_in, b_gate, w_branch_attn, w_pool, pool_scale, w_branch_pool, w_out, ln1_g, ln1_b,
           w_ffn_gate, w_ffn_up, conv_w, conv_b, w_ffn_down, ln2_g, ln2_b):
    depth = w_in.shape[0]
    s = x.shape[1]
    alpha = (2.0 * depth) ** 0.25
    cos, sin = _rope_tables(s)
    vec = lambda t: t.reshape(1, -1)
    for l in range(depth):
        q, k, v, pm, gates = _inproj(x, l, w_in, vec(b_gate[l]), cos, sin,
                                     w_pool[l].astype(BF16), vec(pool_scale[l]))
        y_attn, (w_a, w_p, w_o, w_g, w_u, w_d) = _moba(
            q, k, v, l, [w_branch_attn, w_branch_pool, w_out, w_ffn_gate, w_ffn_up, w_ffn_down])
        x = _mix_ffn(alpha, x, y_attn, pm, gates, w_a, w_p, w_o, vec(ln1_g[l]), vec(ln1_b[l]),
                     w_g, w_u, conv_w[l], vec(conv_b[l]), w_d, vec(ln2_g[l]), vec(ln2_b[l]))
    return x
```

```python
import functools

import jax
import jax.numpy as jnp
from jax import lax
from jax.experimental import pallas as pl
from jax.experimental.pallas import tpu as pltpu

D_MODEL = 1024
ATTN_HEADS = 8
HEAD_DIM = 64
D_ATTN = ATTN_HEADS * HEAD_DIM
MOBA_BLOCK = 256
MOBA_TOPK = 3
ROPE_THETA = 10000.0
POOL_WINDOWS = (2, 4, 8, 16)
POOL_GROUP_DIM = 128
D_POOL = len(POOL_WINDOWS) * POOL_GROUP_DIM
D_FF = 2816
CONV_WIDTH = 3
LN_EPS = 1e-5
NEG = -1e30
Q_SCALE = HEAD_DIM ** -0.5 * 1.4426950408889634

LANES = 128
SUBLANES = 8
HEADS_PER_VREG = LANES // HEAD_DIM
BF16_ROWS = 2 * SUBLANES
PV_ROWS = HEAD_DIM + BF16_ROWS
SCORE_BUFFERS = 3
SCORE_SPLIT = 2
PAIRS_PER_STEP = 2
POOL_HALO = 16
ROW_TILE = 512
FF_CHUNK = 256
LN_ROWS = 256
VMEM_LIMIT = 52 * 1024 * 1024

F32 = jnp.float32
BF16 = jnp.bfloat16


def _dot(a, b):
    return jnp.dot(a, b, preferred_element_type=F32)


def _const_spec(shape):
    return pl.BlockSpec(shape, lambda *_: (0,) * len(shape), pipeline_mode=pl.Buffered(1))


def _layer_norm(y, g, b):
    mu = jnp.mean(y, axis=-1, keepdims=True)
    d = y - mu
    var = jnp.mean(d * d, axis=-1, keepdims=True)
    return d * lax.rsqrt(var + LN_EPS) * g + b


def _inproj_kernel(x_ref, wf_ref, bg_ref, cos_ref, sin_ref, wpool_ref, pscale_ref,
                   q_ref, k_ref, v_ref, pm_ref, g_ref, w_ref, ubuf, *lvl):
    i = pl.program_id(1)
    tm = x_ref.shape[1]

    @pl.when((pl.program_id(0) == 0) & (i == 0))
    def _():
        for c in range(0, w_ref.shape[1], D_ATTN):
            w_ref[:, c:c + D_ATTN] = wf_ref[0, :, c:c + D_ATTN].astype(BF16)

    xb = x_ref[0].astype(BF16)

    lane = lax.broadcasted_iota(jnp.int32, (tm, LANES), 1)
    first_half = (lane % HEAD_DIM) < (HEAD_DIM // 2)
    cos = cos_ref[...]
    sin = sin_ref[...]

    def rope(t):
        partner = jnp.where(first_half, pltpu.roll(t, LANES - HEAD_DIM // 2, 1),
                            pltpu.roll(t, HEAD_DIM // 2, 1))
        return t * cos + partner * sin

    pad, cur, end = SUBLANES, SUBLANES + POOL_HALO, SUBLANES + POOL_HALO + tm

    @pl.when(i == 0)
    def _():
        ubuf[0:cur, :] = jnp.zeros((cur, D_POOL), F32)
        for level in lvl:
            level[0:pad, :] = jnp.zeros((pad, level.shape[1]), F32)

    o3 = 3 * D_ATTN
    ubuf[cur:end, :] = _dot(xb, w_ref[:, o3:o3 + D_POOL])

    zq = _dot(xb, w_ref[:, 0:D_ATTN])
    for c in range(D_ATTN // LANES):
        sl = slice(c * LANES, (c + 1) * LANES)
        q_ref[0, :, sl] = (rope(zq[:, sl]) * Q_SCALE).astype(BF16)
    zk = _dot(xb, w_ref[:, D_ATTN:2 * D_ATTN])
    for c in range(D_ATTN // LANES):
        sl = slice(c * LANES, (c + 1) * LANES)
        k_ref[0, :, sl] = rope(zk[:, sl]).astype(BF16)
    v_ref[0] = _dot(xb, w_ref[:, 2 * D_ATTN:3 * D_ATTN]).astype(BF16)

    o4 = o3 + D_POOL
    for c in range(2 * D_MODEL // D_ATTN):
        sl = slice(c * D_ATTN, (c + 1) * D_ATTN)
        zg = _dot(xb, w_ref[:, o4 + c * D_ATTN:o4 + (c + 1) * D_ATTN]) + bg_ref[:, sl]
        g_ref[0, :, sl] = jax.nn.sigmoid(zg).astype(BF16)

    pos = i * tm + lax.broadcasted_iota(jnp.int32, (tm, 1), 0)
    prev = ubuf
    for g, w in enumerate(POOL_WINDOWS):
        sl = slice(g * POOL_GROUP_DIM, (g + 1) * POOL_GROUP_DIM)
        sums = prev[pad:end, :] + prev[pad - w // 2:end - w // 2, :]
        if g + 1 < len(POOL_WINDOWS):
            lvl[g][pad:end, :] = sums[:, POOL_GROUP_DIM:]
            prev = lvl[g]
        inv_count = 1.0 / jnp.minimum(pos + 1, w).astype(F32)
        pooled = sums[POOL_HALO:, 0:POOL_GROUP_DIM] * inv_count - ubuf[cur:end, sl]
        mixed = _dot(pooled.astype(BF16), wpool_ref[g]) * pscale_ref[:, sl]
        pm_ref[0, :, sl] = mixed.astype(BF16)
    ubuf[pad:cur, :] = ubuf[end - POOL_HALO:end, :]


def _inproj(x, layer, w_in, b_gate, cos, sin, w_pool, pool_scale):
    b, s, d = x.shape
    tm = ROW_TILE
    n_in = w_in.shape[2]
    assert all(w == 2 ** (g + 1) for g, w in enumerate(POOL_WINDOWS)) and POOL_WINDOWS[-1] <= POOL_HALO
    row = lambda n: pl.BlockSpec((1, tm, n), lambda bi, i: (bi, i, 0))
    return pl.pallas_call(
        _inproj_kernel,
        grid=(b, s // tm),
        in_specs=[
            row(d),
            pl.BlockSpec((1, d, n_in), lambda *_: (layer, 0, 0), pipeline_mode=pl.Buffered(1)),
            _const_spec((1, 2 * D_MODEL)),
            pl.BlockSpec((tm, LANES), lambda bi, i: (i, 0)),
            pl.BlockSpec((tm, LANES), lambda bi, i: (i, 0)),
            _const_spec(w_pool.shape),
            _const_spec((1, D_POOL)),
        ],
        out_specs=[row(D_ATTN), row(D_ATTN), row(D_ATTN), row(D_POOL), row(2 * D_MODEL)],
        out_shape=[
            jax.ShapeDtypeStruct((b, s, D_ATTN), BF16),
            jax.ShapeDtypeStruct((b, s, D_ATTN), BF16),
            jax.ShapeDtypeStruct((b, s, D_ATTN), BF16),
            jax.ShapeDtypeStruct((b, s, D_POOL), BF16),
            jax.ShapeDtypeStruct((b, s, 2 * D_MODEL), BF16),
        ],
        scratch_shapes=[pltpu.VMEM((d, n_in), BF16)]
                       + [pltpu.VMEM((SUBLANES + POOL_HALO + tm, D_POOL - g * POOL_GROUP_DIM), F32)
                          for g in range(len(POOL_WINDOWS))],
        compiler_params=pltpu.CompilerParams(
            dimension_semantics=("arbitrary", "arbitrary"), vmem_limit_bytes=VMEM_LIMIT),
        name="inproj",
    )(x, w_in, b_gate, cos, sin, w_pool, pool_scale)


def _moba_kernel(n_cast, q_ref, k_ref, v_ref, *refs):
    w_refs, o_ref, wb_refs = refs[:n_cast], refs[n_cast], refs[n_cast + 1:2 * n_cast + 1]
    khat_ref, vt_ref, qt_ref, *s_refs = refs[2 * n_cast + 1:]
    for w_ref, wb_ref in zip(w_refs, wb_refs):
        wb_ref[...] = w_ref[0].astype(BF16)

    seq = k_ref.shape[1]
    nb = seq // MOBA_BLOCK
    blk = lambda j: slice(j * MOBA_BLOCK, (j + 1) * MOBA_BLOCK)
    lane = lax.broadcasted_iota(jnp.int32, (MOBA_BLOCK, LANES), 1)
    n_pairs = q_ref.shape[2] // LANES
    pair = lambda p: slice(p * LANES, (p + 1) * LANES)

    kmbs = []
    for p in range(n_pairs):
        kf = k_ref[0, :, pair(p)].astype(F32).reshape(nb, MOBA_BLOCK, LANES)
        km = jnp.sum(kf, axis=1) * (1.0 / MOBA_BLOCK)
        hi = km.astype(BF16)
        kmbs.append(jnp.concatenate([hi, (km - hi.astype(F32)).astype(BF16)], axis=0))

        for j in range(nb):
            kj = k_ref[0, blk(j), pair(p)]
            for h in range(HEADS_PER_VREG):
                g = p * HEADS_PER_VREG + h
                onehot = jnp.where(lane == (1 - h) * HEAD_DIM + j, 1.0, 0.0).astype(BF16)
                khat_ref[g, blk(j), :] = jnp.where((lane // HEAD_DIM) == h, kj, onehot)
            vt = v_ref[0, blk(j), pair(p)].astype(F32).T.astype(BF16)
            for h in range(HEADS_PER_VREG):
                g = p * HEADS_PER_VREG + h
                vt_ref[g, 0:HEAD_DIM, blk(j)] = vt[h * HEAD_DIM:(h + 1) * HEAD_DIM]
                vt_ref[g, HEAD_DIM:PV_ROWS, blk(j)] = jnp.ones((PV_ROWS - HEAD_DIM, MOBA_BLOCK), BF16)
            qt_ref[p, :, blk(j)] = q_ref[0, blk(j), pair(p)].astype(F32).T.astype(BF16)

    dim_i = lax.broadcasted_iota(jnp.int32, (LANES, MOBA_BLOCK), 0)
    key_i = lax.broadcasted_iota(jnp.int32, (MOBA_BLOCK, MOBA_BLOCK), 0)
    qry_i = lax.broadcasted_iota(jnp.int32, (MOBA_BLOCK, MOBA_BLOCK), 1)
    causal = key_i <= qry_i
    blk_i = lax.broadcasted_iota(jnp.int32, (nb, MOBA_BLOCK), 0)
    fold = lambda t: t.reshape(MOBA_BLOCK // SUBLANES, SUBLANES, MOBA_BLOCK)

    units = [(i, p * HEADS_PER_VREG + h)
             for p in range(n_pairs) for i in range(nb) for h in range(HEADS_PER_VREG)]
    state = {}
    sbuf = lambda u: s_refs[units.index(u) % len(s_refs)]

    def prepare(u):
        i, (p, h) = u[0], divmod(u[1], HEADS_PER_VREG)
        in_head = (dim_i // HEAD_DIM) == h
        q = qt_ref[p, :, blk(i)]
        if i > MOBA_TOPK:
            g2 = _dot(kmbs[p], jnp.where(in_head, q, jnp.zeros_like(q)))
            gate = g2[0:nb] + g2[nb:2 * nb]
            beaten = jnp.zeros((nb, MOBA_BLOCK), jnp.int32)
            for m in range(i):
                gm = gate[m:m + 1, :]
                wins = (gm > gate) | ((gm == gate) & (m < blk_i))
                beaten = beaten + jnp.where(wins, 1, 0)
            bias = jnp.where((blk_i < i) & (beaten >= MOBA_TOPK), NEG, 0.0)
            base = (1 - h) * HEAD_DIM
            parts = [bias, jnp.zeros((LANES - base - nb, MOBA_BLOCK), F32)]
            if base:
                parts = [jnp.zeros((base, MOBA_BLOCK), F32)] + parts
            spare = jnp.concatenate(parts, axis=0).astype(BF16)
        else:
            spare = jnp.zeros_like(q)
        state[u] = dict(q_aug=jnp.where(in_head, q, spare),
                        m_run=jnp.full((SUBLANES, MOBA_BLOCK), NEG, F32),
                        acc=jnp.zeros((PV_ROWS, MOBA_BLOCK), F32))

    def score_matmul(u):
        q_aug = state[u].pop("q_aug")
        n = u[0] + 1
        for lo in range(0, n, SCORE_SPLIT):
            keys = slice(lo * MOBA_BLOCK, min(lo + SCORE_SPLIT, n) * MOBA_BLOCK)
            sbuf(u)[keys, :] = _dot(khat_ref[u[1], keys, :], q_aug)

    def masked_scores(u, j):
        s = sbuf(u)[blk(j), :]
        return jnp.where(causal, s, NEG) if j == u[0] else s

    def score_step(u, j):
        st = state[u]
        st["m_run"] = jnp.maximum(st["m_run"], jnp.max(fold(masked_scores(u, j)), axis=0))

    def end_scores(u):
        state[u]["m_fin"] = jnp.max(state[u]["m_run"], axis=0, keepdims=True)

    def value_step(u, j):
        st = state[u]
        p = jnp.exp2(masked_scores(u, j) - st["m_fin"])
        st["acc"] = st["acc"] + _dot(vt_ref[u[1], :, blk(j)], p.astype(BF16))

    def finish(u):
        i, (p, h) = u[0], divmod(u[1], HEADS_PER_VREG)
        acc = state.pop(u)["acc"]
        out = acc[0:HEAD_DIM] / acc[HEAD_DIM:HEAD_DIM + 1]
        if h == 0:
            state["head0"] = out
        else:
            both = jnp.concatenate([state.pop("head0"), out], axis=0)
            o_ref[0, blk(i), pair(p)] = both.T.astype(BF16)

    def stage_tasks(t):
        unit = lambda d: units[t + d] if 0 <= t + d < len(units) else None
        stages = []
        if unit(3):
            stages.append([functools.partial(prepare, unit(3))])
        if unit(2):
            stages.append([functools.partial(score_matmul, unit(2))])
        if unit(1):
            stages.append([functools.partial(score_step, unit(1), j) for j in range(unit(1)[0] + 1)]
                          + [functools.partial(end_scores, unit(1))])
        if unit(0):
            stages.append([functools.partial(value_step, unit(0), j) for j in range(unit(0)[0] + 1)])
        return stages

    for t in range(-3, len(units)):
        stages = stage_tasks(t)
        n = max(len(tasks) for tasks in stages)
        for step in range(n):
            for tasks in stages:
                lo, hi_ = step * len(tasks) // n, (step + 1) * len(tasks) // n
                for task in tasks[lo:hi_]:
                    task()
        if t >= 0:
            finish(units[t])


def _cast_rows(rows, steps):
    r = -(-rows // steps)
    r = -(-r // BF16_ROWS) * BF16_ROWS
    while rows % r:
        r += BF16_ROWS
    return r


def _moba(q, k, v, layer, weights):
    b, s, _ = q.shape
    width = PAIRS_PER_STEP * LANES
    n_steps = D_ATTN // width
    full = pl.BlockSpec((1, s, width), lambda bi, p: (bi, 0, p))
    w_in_specs, w_out_specs, w_out_shapes = [], [], []
    for w in weights:
        _, rows, cols = w.shape
        r = _cast_rows(rows, b * n_steps)
        last = rows // r - 1
        w_in_specs.append(pl.BlockSpec(
            (1, r, cols), lambda bi, p, last=last: (layer, jnp.minimum(bi * n_steps + p, last), 0)))
        w_out_specs.append(pl.BlockSpec(
            (r, cols), lambda bi, p, last=last: (jnp.minimum(bi * n_steps + p, last), 0)))
        w_out_shapes.append(jax.ShapeDtypeStruct((rows, cols), BF16))
    y_attn, *w_bf16 = pl.pallas_call(
        functools.partial(_moba_kernel, len(weights)),
        grid=(b, n_steps),
        in_specs=[full, full, full] + w_in_specs,
        out_specs=[full] + w_out_specs,
        out_shape=[jax.ShapeDtypeStruct((b, s, D_ATTN), BF16)] + w_out_shapes,
        scratch_shapes=[pltpu.VMEM((PAIRS_PER_STEP * HEADS_PER_VREG, s, LANES), BF16),
                        pltpu.VMEM((PAIRS_PER_STEP * HEADS_PER_VREG, PV_ROWS, s), BF16),
                        pltpu.VMEM((PAIRS_PER_STEP, LANES, s), BF16),
                        *[pltpu.VMEM((s, MOBA_BLOCK), F32)] * SCORE_BUFFERS],
        compiler_params=pltpu.CompilerParams(
            dimension_semantics=("arbitrary", "arbitrary"), vmem_limit_bytes=VMEM_LIMIT),
        name="moba",
    )(q, k, v, *weights)
    return y_attn, w_bf16


def _mix_ffn_kernel(alpha, x_ref, ya_ref, pm_ref, g_ref, wa_ref, wp_ref, wo_ref, l1g_ref, l1b_ref,
                    wg_ref, wu_ref, cw_ref, cb_ref, wd_ref, l2g_ref, l2b_ref, o_ref,
                    x1buf, abuf, hbuf):
    i = pl.program_id(1)
    tm = x_ref.shape[1]
    halo = SUBLANES
    groups = [slice(r * LN_ROWS, (r + 1) * LN_ROWS) for r in range(tm // LN_ROWS)]

    @pl.when(i == 0)
    def _():
        abuf[0:halo, :] = jnp.zeros((halo, D_FF), F32)

    branches = [(_dot(ya_ref[0, rows, :], wa_ref[...]), _dot(pm_ref[0, rows, :], wp_ref[...]))
                for rows in groups]
    for rows, (y_attn, y_pool) in zip(groups, branches):
        g_attn = g_ref[0, rows, 0:D_MODEL].astype(F32)
        g_pool = g_ref[0, rows, D_MODEL:2 * D_MODEL].astype(F32)
        mix = _dot((g_attn * y_attn + g_pool * y_pool).astype(BF16), wo_ref[...])
        x1buf[rows, :] = _layer_norm(alpha * x_ref[0, rows, :] + mix, l1g_ref[...], l1b_ref[...])

    xb = x1buf[...].astype(BF16)
    for c in range(D_FF // FF_CHUNK):
        sl = slice(c * FF_CHUNK, (c + 1) * FF_CHUNK)
        if c == 0:
            parts = [(_dot(xg, wg_ref[:, sl]), _dot(xg, wu_ref[:, sl]))
                     for xg in (x1buf[rows, :].astype(BF16) for rows in groups)]
            a = jnp.concatenate([p[0] for p in parts])
            u = jnp.concatenate([p[1] for p in parts])
        else:
            a = _dot(xb, wg_ref[:, sl])
            u = _dot(xb, wu_ref[:, sl])
        abuf[halo:halo + tm, sl] = a
        conv = a * cw_ref[CONV_WIDTH - 1:CONV_WIDTH, sl] + cb_ref[:, sl]
        for t in range(CONV_WIDTH - 1):
            back = CONV_WIDTH - 1 - t
            conv = conv + abuf[halo - back:halo - back + tm, sl] * cw_ref[t:t + 1, sl]
        abuf[0:halo, sl] = a[tm - halo:tm, :]
        gelu = 0.5 * conv * (1.0 + lax.erf(conv * (2.0 ** -0.5)))
        hbuf[:, sl] = (gelu * u).astype(BF16)
    for rows in groups:
        ffn = _dot(hbuf[rows, :], wd_ref[...])
        o_ref[0, rows, :] = _layer_norm(alpha * x1buf[rows, :] + ffn, l2g_ref[...], l2b_ref[...])


def _mix_ffn(alpha, x, y_attn, pm, gates, w_a, w_p, w_o, ln1_g, ln1_b,
             w_g, w_u, conv_w, conv_b, w_d, ln2_g, ln2_b):
    b, s, d = x.shape
    tm = ROW_TILE
    row = lambda n: pl.BlockSpec((1, tm, n), lambda bi, i: (bi, i, 0))
    consts = [w_a, w_p, w_o, ln1_g, ln1_b, w_g, w_u, conv_w, conv_b, w_d, ln2_g, ln2_b]
    return pl.pallas_call(
        functools.partial(_mix_ffn_kernel, alpha),
        grid=(b, s // tm),
        in_specs=[row(d), row(D_ATTN), row(D_POOL), row(2 * D_MODEL)]
                 + [_const_spec(c.shape) for c in consts],
        out_specs=row(d),
        out_shape=jax.ShapeDtypeStruct((b, s, d), F32),
        scratch_shapes=[pltpu.VMEM((tm, d), F32),
                        pltpu.VMEM((SUBLANES + tm, D_FF), F32),
                        pltpu.VMEM((tm, D_FF), BF16)],
        compiler_params=pltpu.CompilerParams(
            dimension_semantics=("arbitrary", "arbitrary"), vmem_limit_bytes=VMEM_LIMIT),
        name="mix_ffn",
    )(x, y_attn, pm, gates, *consts)


def _rope_tables(s):
    half = HEAD_DIM // 2
    inv_freq = 1.0 / (ROPE_THETA ** (jnp.arange(half, dtype=F32) / half))
    ang = jnp.arange(s, dtype=F32)[:, None] * inv_freq[None, :]
    cos, sin = jnp.cos(ang), jnp.sin(ang)
    cos = jnp.tile(jnp.concatenate([cos, cos], axis=-1), (1, HEADS_PER_VREG))
    sin = jnp.tile(jnp.concatenate([-sin, sin], axis=-1), (1, HEADS_PER_VREG))
    return cos, sin


def kernel(x, w_in, b_gate, w_branch_attn, w_pool, pool_scale, w_branch_pool, w_out, ln1_g, ln1_b,
           w_ffn_gate, w_ffn_up, conv_w, conv_b, w_ffn_down, ln2_g, ln2_b):
    depth = w_in.shape[0]
    s = x.shape[1]
    alpha = (2.0 * depth) ** 0.25
    cos, sin = _rope_tables(s)
    vec = lambda t: t.reshape(1, -1)
    for l in range(depth):
        q, k, v, pm, gates = _inproj(x, l, w_in, vec(b_gate[l]), cos, sin,
                                     w_pool[l].astype(BF16), vec(pool_scale[l]))
        y_attn, (w_a, w_p, w_o, w_g, w_u, w_d) = _moba(
            q, k, v, l, [w_branch_attn, w_branch_pool, w_out, w_ffn_gate, w_ffn_up, w_ffn_down])
        x = _mix_ffn(alpha, x, y_attn, pm, gates, w_a, w_p, w_o, vec(ln1_g[l]), vec(ln1_b[l]),
                     w_g, w_u, conv_w[l], vec(conv_b[l]), w_d, vec(ln2_g[l]), vec(ln2_b[l]))
    return x
```

```python
import functools

import jax
import jax.numpy as jnp
from jax import lax
from jax.experimental import pallas as pl
from jax.experimental.pallas import tpu as pltpu

D_MODEL = 1024
ATTN_HEADS = 8
HEAD_DIM = 64
D_ATTN = ATTN_HEADS * HEAD_DIM
MOBA_BLOCK = 256
MOBA_TOPK = 3
ROPE_THETA = 10000.0
POOL_WINDOWS = (2, 4, 8, 16)
POOL_GROUP_DIM = 128
D_POOL = len(POOL_WINDOWS) * POOL_GROUP_DIM
D_FF = 2816
CONV_WIDTH = 3
LN_EPS = 1e-5
NEG = -1e30
Q_SCALE = HEAD_DIM ** -0.5 * 1.4426950408889634

LANES = 128
SUBLANES = 8
HEADS_PER_VREG = LANES // HEAD_DIM
BF16_ROWS = 2 * SUBLANES
PV_ROWS = HEAD_DIM + BF16_ROWS
SCORE_BUFFERS = 3
SCORE_SPLIT = 2
PAIRS_PER_STEP = 2
POOL_HALO = 16
ROW_TILE = 512
FF_CHUNK = 256
LN_ROWS = 256
VMEM_LIMIT = 52 * 1024 * 1024

F32 = jnp.float32
BF16 = jnp.bfloat16


def _dot(a, b):
    return jnp.dot(a, b, preferred_element_type=F32)


def _const_spec(shape):
    return pl.BlockSpec(shape, lambda *_: (0,) * len(shape), pipeline_mode=pl.Buffered(1))


def _layer_norm(y, g, b):
    mu = jnp.mean(y, axis=-1, keepdims=True)
    d = y - mu
    var = jnp.mean(d * d, axis=-1, keepdims=True)
    return d * lax.rsqrt(var + LN_EPS) * g + b


def _inproj_kernel(x_ref, wf_ref, bg_ref, cos_ref, sin_ref, wpool_ref, pscale_ref,
                   q_ref, k_ref, v_ref, pm_ref, g_ref, w_ref, ubuf, *lvl):
    i = pl.program_id(1)
    tm = x_ref.shape[1]

    @pl.when((pl.program_id(0) == 0) & (i == 0))
    def _():
        for c in range(0, w_ref.shape[1], D_ATTN):
            w_ref[:, c:c + D_ATTN] = wf_ref[0, :, c:c + D_ATTN].astype(BF16)

    xb = x_ref[0].astype(BF16)

    lane = lax.broadcasted_iota(jnp.int32, (tm, LANES), 1)
    first_half = (lane % HEAD_DIM) < (HEAD_DIM // 2)
    cos = cos_ref[...]
    sin = sin_ref[...]

    def rope(t):
        partner = jnp.where(first_half, pltpu.roll(t, LANES - HEAD_DIM // 2, 1),
                            pltpu.roll(t, HEAD_DIM // 2, 1))
        return t * cos + partner * sin

    pad, cur, end = SUBLANES, SUBLANES + POOL_HALO, SUBLANES + POOL_HALO + tm

    @pl.when(i == 0)
    def _():
        ubuf[0:cur, :] = jnp.zeros((cur, D_POOL), F32)
        for level in lvl:
            level[0:pad, :] = jnp.zeros((pad, level.shape[1]), F32)

    o3 = 3 * D_ATTN
    ubuf[cur:end, :] = _dot(xb, w_ref[:, o3:o3 + D_POOL])

    zq = _dot(xb, w_ref[:, 0:D_ATTN])
    for c in range(D_ATTN // LANES):
        sl = slice(c * LANES, (c + 1) * LANES)
        q_ref[0, :, sl] = (rope(zq[:, sl]) * Q_SCALE).astype(BF16)
    zk = _dot(xb, w_ref[:, D_ATTN:2 * D_ATTN])
    for c in range(D_ATTN // LANES):
        sl = slice(c * LANES, (c + 1) * LANES)
        k_ref[0, :, sl] = rope(zk[:, sl]).astype(BF16)
    v_ref[0] = _dot(xb, w_ref[:, 2 * D_ATTN:3 * D_ATTN]).astype(BF16)

    o4 = o3 + D_POOL
    for c in range(2 * D_MODEL // D_ATTN):
        sl = slice(c * D_ATTN, (c + 1) * D_ATTN)
        zg = _dot(xb, w_ref[:, o4 + c * D_ATTN:o4 + (c + 1) * D_ATTN]) + bg_ref[:, sl]
        g_ref[0, :, sl] = jax.nn.sigmoid(zg).astype(BF16)

    pos = i * tm + lax.broadcasted_iota(jnp.int32, (tm, 1), 0)
    prev = ubuf
    for g, w in enumerate(POOL_WINDOWS):
        sl = slice(g * POOL_GROUP_DIM, (g + 1) * POOL_GROUP_DIM)
        sums = prev[pad:end, :] + prev[pad - w // 2:end - w // 2, :]
        if g + 1 < len(POOL_WINDOWS):
            lvl[g][pad:end, :] = sums[:, POOL_GROUP_DIM:]
            prev = lvl[g]
        inv_count = 1.0 / jnp.minimum(pos + 1, w).astype(F32)
        pooled = sums[POOL_HALO:, 0:POOL_GROUP_DIM] * inv_count - ubuf[cur:end, sl]
        mixed = _dot(pooled.astype(BF16), wpool_ref[g]) * pscale_ref[:, sl]
        pm_ref[0, :, sl] = mixed.astype(BF16)
    ubuf[pad:cur, :] = ubuf[end - POOL_HALO:end, :]


def _inproj(x, layer, w_in, b_gate, cos, sin, w_pool, pool_scale):
    b, s, d = x.shape
    tm = ROW_TILE
    n_in = w_in.shape[2]
    assert all(w == 2 ** (g + 1) for g, w in enumerate(POOL_WINDOWS)) and POOL_WINDOWS[-1] <= POOL_HALO
    row = lambda n: pl.BlockSpec((1, tm, n), lambda bi, i: (bi, i, 0))
    return pl.pallas_call(
        _inproj_kernel,
        grid=(b, s // tm),
        in_specs=[
            row(d),
            pl.BlockSpec((1, d, n_in), lambda *_: (layer, 0, 0), pipeline_mode=pl.Buffered(1)),
            _const_spec((1, 2 * D_MODEL)),
            pl.BlockSpec((tm, LANES), lambda bi, i: (i, 0)),
            pl.BlockSpec((tm, LANES), lambda bi, i: (i, 0)),
            _const_spec(w_pool.shape),
            _const_spec((1, D_POOL)),
        ],
        out_specs=[row(D_ATTN), row(D_ATTN), row(D_ATTN), row(D_POOL), row(2 * D_MODEL)],
        out_shape=[
            jax.ShapeDtypeStruct((b, s, D_ATTN), BF16),
            jax.ShapeDtypeStruct((b, s, D_ATTN), BF16),
            jax.ShapeDtypeStruct((b, s, D_ATTN), BF16),
            jax.ShapeDtypeStruct((b, s, D_POOL), BF16),
            jax.ShapeDtypeStruct((b, s, 2 * D_MODEL), BF16),
        ],
        scratch_shapes=[pltpu.VMEM((d, n_in), BF16)]
                       + [pltpu.VMEM((SUBLANES + POOL_HALO + tm, D_POOL - g * POOL_GROUP_DIM), F32)
                          for g in range(len(POOL_WINDOWS))],
        compiler_params=pltpu.CompilerParams(
            dimension_semantics=("arbitrary", "arbitrary"), vmem_limit_bytes=VMEM_LIMIT),
        name="inproj",
    )(x, w_in, b_gate, cos, sin, w_pool, pool_scale)


def _moba_kernel(n_cast, q_ref, k_ref, v_ref, *refs):
    w_refs, o_ref, wb_refs = refs[:n_cast], refs[n_cast], refs[n_cast + 1:2 * n_cast + 1]
    khat_ref, vt_ref, qt_ref, *s_refs = refs[2 * n_cast + 1:]
    for w_ref, wb_ref in zip(w_refs, wb_refs):
        wb_ref[...] = w_ref[0].astype(BF16)

    seq = k_ref.shape[1]
    nb = seq // MOBA_BLOCK
    blk = lambda j: slice(j * MOBA_BLOCK, (j + 1) * MOBA_BLOCK)
    lane = lax.broadcasted_iota(jnp.int32, (MOBA_BLOCK, LANES), 1)
    n_pairs = q_ref.shape[2] // LANES
    pair = lambda p: slice(p * LANES, (p + 1) * LANES)

    kmbs = []
    for p in range(n_pairs):
        kf = k_ref[0, :, pair(p)].astype(F32).reshape(nb, MOBA_BLOCK, LANES)
        km = jnp.sum(kf, axis=1) * (1.0 / MOBA_BLOCK)
        hi = km.astype(BF16)
        kmbs.append(jnp.concatenate([hi, (km - hi.astype(F32)).astype(BF16)], axis=0))

        for j in range(nb):
            kj = k_ref[0, blk(j), pair(p)]
            for h in range(HEADS_PER_VREG):
                g = p * HEADS_PER_VREG + h
                onehot = jnp.where(lane == (1 - h) * HEAD_DIM + j, 1.0, 0.0).astype(BF16)
                khat_ref[g, blk(j), :] = jnp.where((lane // HEAD_DIM) == h, kj, onehot)
            vt = v_ref[0, blk(j), pair(p)].astype(F32).T.astype(BF16)
            for h in range(HEADS_PER_VREG):
                g = p * HEADS_PER_VREG + h
                vt_ref[g, 0:HEAD_DIM, blk(j)] = vt[h * HEAD_DIM:(h + 1) * HEAD_DIM]
                vt_ref[g, HEAD_DIM:PV_ROWS, blk(j)] = jnp.ones((PV_ROWS - HEAD_DIM, MOBA_BLOCK), BF16)
            qt_ref[p, :, blk(j)] = q_ref[0, blk(j), pair(p)].astype(F32).T.astype(BF16)

    dim_i = lax.broadcasted_iota(jnp.int32, (LANES, MOBA_BLOCK), 0)
    key_i = lax.broadcasted_iota(jnp.int32, (MOBA_BLOCK, MOBA_BLOCK), 0)
    qry_i = lax.broadcasted_iota(jnp.int32, (MOBA_BLOCK, MOBA_BLOCK), 1)
    causal = key_i <= qry_i
    blk_i = lax.broadcasted_iota(jnp.int32, (nb, MOBA_BLOCK), 0)
    fold = lambda t: t.reshape(MOBA_BLOCK // SUBLANES, SUBLANES, MOBA_BLOCK)

    units = [(i, p * HEADS_PER_VREG + h)
             for p in range(n_pairs) for i in range(nb) for h in range(HEADS_PER_VREG)]
    state = {}
    sbuf = lambda u: s_refs[units.index(u) % len(s_refs)]

    def prepare(u):
        i, (p, h) = u[0], divmod(u[1], HEADS_PER_VREG)
        in_head = (dim_i // HEAD_DIM) == h
        q = qt_ref[p, :, blk(i)]
        if i > MOBA_TOPK:
            g2 = _dot(kmbs[p], jnp.where(in_head, q, jnp.zeros_like(q)))
            gate = g2[0:nb] + g2[nb:2 * nb]
            beaten = jnp.zeros((nb, MOBA_BLOCK), jnp.int32)
            for m in range(i):
                gm = gate[m:m + 1, :]
                wins = (gm > gate) | ((gm == gate) & (m < blk_i))
                beaten = beaten + jnp.where(wins, 1, 0)
            bias = jnp.where((blk_i < i) & (beaten >= MOBA_TOPK), NEG, 0.0)
            base = (1 - h) * HEAD_DIM
            parts = [bias, jnp.zeros((LANES - base - nb, MOBA_BLOCK), F32)]
            if base:
                parts = [jnp.zeros((base, MOBA_BLOCK), F32)] + parts
            spare = jnp.concatenate(parts, axis=0).astype(BF16)
        else:
            spare = jnp.zeros_like(q)
        state[u] = dict(q_aug=jnp.where(in_head, q, spare),
                        m_run=jnp.full((SUBLANES, MOBA_BLOCK), NEG, F32),
                        acc=jnp.zeros((PV_ROWS, MOBA_BLOCK), F32))

    def score_matmul(u):
        q_aug = state[u].pop("q_aug")
        n = u[0] + 1
        for lo in range(0, n, SCORE_SPLIT):
            keys = slice(lo * MOBA_BLOCK, min(lo + SCORE_SPLIT, n) * MOBA_BLOCK)
            sbuf(u)[keys, :] = _dot(khat_ref[u[1], keys, :], q_aug)

    def masked_scores(u, j):
        s = sbuf(u)[blk(j), :]
        return jnp.where(causal, s, NEG) if j == u[0] else s

    def score_step(u, j):
        st = state[u]
        st["m_run"] = jnp.maximum(st["m_run"], jnp.max(fold(masked_scores(u, j)), axis=0))

    def end_scores(u):
        state[u]["m_fin"] = jnp.max(state[u]["m_run"], axis=0, keepdims=True)

    def value_step(u, j):
        st = state[u]
        p = jnp.exp2(masked_scores(u, j) - st["m_fin"])
        st["acc"] = st["acc"] + _dot(vt_ref[u[1], :, blk(j)], p.astype(BF16))

    def finish(u):
        i, (p, h) = u[0], divmod(u[1], HEADS_PER_VREG)
        acc = state.pop(u)["acc"]
        out = acc[0:HEAD_DIM] / acc[HEAD_DIM:HEAD_DIM + 1]
        if h == 0:
            state["head0"] = out
        else:
            both = jnp.concatenate([state.pop("head0"), out], axis=0)
            o_ref[0, blk(i), pair(p)] = both.T.astype(BF16)

    def stage_tasks(t):
        unit = lambda d: units[t + d] if 0 <= t + d < len(units) else None
        stages = []
        if unit(3):
            stages.append([functools.partial(prepare, unit(3))])
        if unit(2):
            stages.append([functools.partial(score_matmul, unit(2))])
        if unit(1):
            stages.append([functools.partial(score_step, unit(1), j) for j in range(unit(1)[0] + 1)]
                          + [functools.partial(end_scores, unit(1))])
        if unit(0):
            stages.append([functools.partial(value_step, unit(0), j) for j in range(unit(0)[0] + 1)])
        return stages

    for t in range(-3, len(units)):
        stages = stage_tasks(t)
        n = max(len(tasks) for tasks in stages)
        for step in range(n):
            for tasks in stages:
                lo, hi_ = step * len(tasks) // n, (step + 1) * len(tasks) // n
                for task in tasks[lo:hi_]:
                    task()
        if t >= 0:
            finish(units[t])


def _cast_rows(rows, steps):
    r = -(-rows // steps)
    r = -(-r // BF16_ROWS) * BF16_ROWS
    while rows % r:
        r += BF16_ROWS
    return r


def _moba(q, k, v, layer, weights):
    b, s, _ = q.shape
    width = PAIRS_PER_STEP * LANES
    n_steps = D_ATTN // width
    full = pl.BlockSpec((1, s, width), lambda bi, p: (bi, 0, p))
    w_in_specs, w_out_specs, w_out_shapes = [], [], []
    for w in weights:
        _, rows, cols = w.shape
        r = _cast_rows(rows, b * n_steps)
        last = rows // r - 1
        w_in_specs.append(pl.BlockSpec(
            (1, r, cols), lambda bi, p, last=last: (layer, jnp.minimum(bi * n_steps + p, last), 0)))
        w_out_specs.append(pl.BlockSpec(
            (r, cols), lambda bi, p, last=last: (jnp.minimum(bi * n_steps + p, last), 0)))
        w_out_shapes.append(jax.ShapeDtypeStruct((rows, cols), BF16))
    y_attn, *w_bf16 = pl.pallas_call(
        functools.partial(_moba_kernel, len(weights)),
        grid=(b, n_steps),
        in_specs=[full, full, full] + w_in_specs,
        out_specs=[full] + w_out_specs,
        out_shape=[jax.ShapeDtypeStruct((b, s, D_ATTN), BF16)] + w_out_shapes,
        scratch_shapes=[pltpu.VMEM((PAIRS_PER_STEP * HEADS_PER_VREG, s, LANES), BF16),
                        pltpu.VMEM((PAIRS_PER_STEP * HEADS_PER_VREG, PV_ROWS, s), BF16),
                        pltpu.VMEM((PAIRS_PER_STEP, LANES, s), BF16),
                        *[pltpu.VMEM((s, MOBA_BLOCK), F32)] * SCORE_BUFFERS],
        compiler_params=pltpu.CompilerParams(
            dimension_semantics=("arbitrary", "arbitrary"), vmem_limit_bytes=VMEM_LIMIT),
        name="moba",
    )(q, k, v, *weights)
    return y_attn, w_bf16


def _mix_ffn_kernel(alpha, x_ref, ya_ref, pm_ref, g_ref, wa_ref, wp_ref, wo_ref, l1g_ref, l1b_ref,
                    wg_ref, wu_ref, cw_ref, cb_ref, wd_ref, l2g_ref, l2b_ref, o_ref,
                    x1buf, abuf, hbuf):
    i = pl.program_id(1)
    tm = x_ref.shape[1]
    halo = SUBLANES
    groups = [slice(r * LN_ROWS, (r + 1) * LN_ROWS) for r in range(tm // LN_ROWS)]

    @pl.when(i == 0)
    def _():
        abuf[0:halo, :] = jnp.zeros((halo, D_FF), F32)

    branches = [(_dot(ya_ref[0, rows, :], wa_ref[...]), _dot(pm_ref[0, rows, :], wp_ref[...]))
                for rows in groups]
    for rows, (y_attn, y_pool) in zip(groups, branches):
        g_attn = g_ref[0, rows, 0:D_MODEL].astype(F32)
        g_pool = g_ref[0, rows, D_MODEL:2 * D_MODEL].astype(F32)
        mix = _dot((g_attn * y_attn + g_pool * y_pool).astype(BF16), wo_ref[...])
        x1buf[rows, :] = _layer_norm(alpha * x_ref[0, rows, :] + mix, l1g_ref[...], l1b_ref[...])

    xb = x1buf[...].astype(BF16)
    for c in range(D_FF // FF_CHUNK):
        sl = slice(c * FF_CHUNK, (c + 1) * FF_CHUNK)
        for rows in groups:
            xg = x1buf[rows, :].astype(BF16) if c == 0 else xb[rows]
            a = _dot(xg, wg_ref[:, sl])
            u = _dot(xg, wu_ref[:, sl])
            lo = halo + rows.start
            abuf[lo:lo + LN_ROWS, sl] = a
            conv = a * cw_ref[CONV_WIDTH - 1:CONV_WIDTH, sl] + cb_ref[:, sl]
            for t in range(CONV_WIDTH - 1):
                back = CONV_WIDTH - 1 - t
                conv = conv + abuf[lo - back:lo - back + LN_ROWS, sl] * cw_ref[t:t + 1, sl]
            gelu = 0.5 * conv * (1.0 + lax.erf(conv * (2.0 ** -0.5)))
            hbuf[rows, sl] = (gelu * u).astype(BF16)
        abuf[0:halo, sl] = abuf[tm:tm + halo, sl]
    for rows in groups:
        ffn = _dot(hbuf[rows, :], wd_ref[...])
        o_ref[0, rows, :] = _layer_norm(alpha * x1buf[rows, :] + ffn, l2g_ref[...], l2b_ref[...])


def _mix_ffn(alpha, x, y_attn, pm, gates, w_a, w_p, w_o, ln1_g, ln1_b,
             w_g, w_u, conv_w, conv_b, w_d, ln2_g, ln2_b):
    b, s, d = x.shape
    tm = ROW_TILE
    row = lambda n: pl.BlockSpec((1, tm, n), lambda bi, i: (bi, i, 0))
    consts = [w_a, w_p, w_o, ln1_g, ln1_b, w_g, w_u, conv_w, conv_b, w_d, ln2_g, ln2_b]
    return pl.pallas_call(
        functools.partial(_mix_ffn_kernel, alpha),
        grid=(b, s // tm),
        in_specs=[row(d), row(D_ATTN), row(D_POOL), row(2 * D_MODEL)]
                 + [_const_spec(c.shape) for c in consts],
        out_specs=row(d),
        out_shape=jax.ShapeDtypeStruct((b, s, d), F32),
        scratch_shapes=[pltpu.VMEM((tm, d), F32),
                        pltpu.VMEM((SUBLANES + tm, D_FF), F32),
                        pltpu.VMEM((tm, D_FF), BF16)],
        compiler_params=pltpu.CompilerParams(
            dimension_semantics=("arbitrary", "arbitrary"), vmem_limit_bytes=VMEM_LIMIT),
        name="mix_ffn",
    )(x, y_attn, pm, gates, *consts)


def _rope_tables(s):
    half = HEAD_DIM // 2
    inv_freq = 1.0 / (ROPE_THETA ** (jnp.arange(half, dtype=F32) / half))
    ang = jnp.arange(s, dtype=F32)[:, None] * inv_freq[None, :]
    cos, sin = jnp.cos(ang), jnp.sin(ang)
    cos = jnp.tile(jnp.concatenate([cos, cos], axis=-1), (1, HEADS_PER_VREG))
    sin = jnp.tile(jnp.concatenate([-sin, sin], axis=-1), (1, HEADS_PER_VREG))
    return cos, sin


def kernel(x, w_in, b_gate, w_branch_attn, w_pool, pool_scale, w_branch_pool, w_out, ln1_g, ln1_b,
           w_ffn_gate, w_ffn_up, conv_w, conv_b, w_ffn_down, ln2_g, ln2_b):
    depth = w_in.shape[0]
    s = x.shape[1]
    alpha = (2.0 * depth) ** 0.25
    cos, sin = _rope_tables(s)
    vec = lambda t: t.reshape(1, -1)
    for l in range(depth):
        q, k, v, pm, gates = _inproj(x, l, w_in, vec(b_gate[l]), cos, sin,
                                     w_pool[l].astype(BF16), vec(pool_scale[l]))
        y_attn, (w_a, w_p, w_o, w_g, w_u, w_d) = _moba(
            q, k, v, l, [w_branch_attn, w_branch_pool, w_out, w_ffn_gate, w_ffn_up, w_ffn_down])
        x = _mix_ffn(alpha, x, y_attn, pm, gates, w_a, w_p, w_o, vec(ln1_g[l]), vec(ln1_b[l]),
                     w_g, w_u, conv_w[l], vec(conv_b[l]), w_d, vec(ln2_g[l]), vec(ln2_b[l]))
    return x
```

```python
import functools

import jax
import jax.numpy as jnp
from jax import lax
from jax.experimental import pallas as pl
from jax.experimental.pallas import tpu as pltpu

D_MODEL = 1024
ATTN_HEADS = 8
HEAD_DIM = 64
D_ATTN = ATTN_HEADS * HEAD_DIM
MOBA_BLOCK = 256
MOBA_TOPK = 3
ROPE_THETA = 10000.0
POOL_WINDOWS = (2, 4, 8, 16)
POOL_GROUP_DIM = 128
D_POOL = len(POOL_WINDOWS) * POOL_GROUP_DIM
D_FF = 2816
CONV_WIDTH = 3
LN_EPS = 1e-5
NEG = -1e30
Q_SCALE = HEAD_DIM ** -0.5 * 1.4426950408889634

LANES = 128
SUBLANES = 8
HEADS_PER_VREG = LANES // HEAD_DIM
BF16_ROWS = 2 * SUBLANES
PV_ROWS = HEAD_DIM + BF16_ROWS
SCORE_BUFFERS = 3
SCORE_SPLIT = 2
PAIRS_PER_STEP = 4
POOL_HALO = 16
ROW_TILE = 512
FF_CHUNK = 256
LN_ROWS = 256
VMEM_LIMIT = 52 * 1024 * 1024

F32 = jnp.float32
BF16 = jnp.bfloat16


def _dot(a, b):
    return jnp.dot(a, b, preferred_element_type=F32)


def _const_spec(shape):
    return pl.BlockSpec(shape, lambda *_: (0,) * len(shape), pipeline_mode=pl.Buffered(1))


def _layer_norm(y, g, b):
    mu = jnp.mean(y, axis=-1, keepdims=True)
    d = y - mu
    var = jnp.mean(d * d, axis=-1, keepdims=True)
    return d * lax.rsqrt(var + LN_EPS) * g + b


def _inproj_kernel(x_ref, wf_ref, bg_ref, cos_ref, sin_ref, wpool_ref, pscale_ref,
                   q_ref, k_ref, v_ref, pm_ref, g_ref, w_ref, ubuf, *lvl):
    i = pl.program_id(1)
    tm = x_ref.shape[1]

    @pl.when((pl.program_id(0) == 0) & (i == 0))
    def _():
        for c in range(0, w_ref.shape[1], D_ATTN):
            w_ref[:, c:c + D_ATTN] = wf_ref[0, :, c:c + D_ATTN].astype(BF16)

    xb = x_ref[0].astype(BF16)

    lane = lax.broadcasted_iota(jnp.int32, (tm, LANES), 1)
    first_half = (lane % HEAD_DIM) < (HEAD_DIM // 2)
    cos = cos_ref[...]
    sin = sin_ref[...]

    def rope(t):
        partner = jnp.where(first_half, pltpu.roll(t, LANES - HEAD_DIM // 2, 1),
                            pltpu.roll(t, HEAD_DIM // 2, 1))
        return t * cos + partner * sin

    pad, cur, end = SUBLANES, SUBLANES + POOL_HALO, SUBLANES + POOL_HALO + tm

    @pl.when(i == 0)
    def _():
        ubuf[0:cur, :] = jnp.zeros((cur, D_POOL), F32)
        for level in lvl:
            level[0:pad, :] = jnp.zeros((pad, level.shape[1]), F32)

    o3 = 3 * D_ATTN
    ubuf[cur:end, :] = _dot(xb, w_ref[:, o3:o3 + D_POOL])

    zq = _dot(xb, w_ref[:, 0:D_ATTN])
    for c in range(D_ATTN // LANES):
        sl = slice(c * LANES, (c + 1) * LANES)
        q_ref[0, :, sl] = (rope(zq[:, sl]) * Q_SCALE).astype(BF16)
    zk = _dot(xb, w_ref[:, D_ATTN:2 * D_ATTN])
    for c in range(D_ATTN // LANES):
        sl = slice(c * LANES, (c + 1) * LANES)
        k_ref[0, :, sl] = rope(zk[:, sl]).astype(BF16)
    v_ref[0] = _dot(xb, w_ref[:, 2 * D_ATTN:3 * D_ATTN]).astype(BF16)

    o4 = o3 + D_POOL
    for c in range(2 * D_MODEL // D_ATTN):
        sl = slice(c * D_ATTN, (c + 1) * D_ATTN)
        zg = _dot(xb, w_ref[:, o4 + c * D_ATTN:o4 + (c + 1) * D_ATTN]) + bg_ref[:, sl]
        g_ref[0, :, sl] = jax.nn.sigmoid(zg).astype(BF16)

    pos = i * tm + lax.broadcasted_iota(jnp.int32, (tm, 1), 0)
    prev = ubuf
    for g, w in enumerate(POOL_WINDOWS):
        sl = slice(g * POOL_GROUP_DIM, (g + 1) * POOL_GROUP_DIM)
        sums = prev[pad:end, :] + prev[pad - w // 2:end - w // 2, :]
        if g + 1 < len(POOL_WINDOWS):
            lvl[g][pad:end, :] = sums[:, POOL_GROUP_DIM:]
            prev = lvl[g]
        inv_count = 1.0 / jnp.minimum(pos + 1, w).astype(F32)
        pooled = sums[POOL_HALO:, 0:POOL_GROUP_DIM] * inv_count - ubuf[cur:end, sl]
        mixed = _dot(pooled.astype(BF16), wpool_ref[g]) * pscale_ref[:, sl]
        pm_ref[0, :, sl] = mixed.astype(BF16)
    ubuf[pad:cur, :] = ubuf[end - POOL_HALO:end, :]


def _inproj(x, layer, w_in, b_gate, cos, sin, w_pool, pool_scale):
    b, s, d = x.shape
    tm = ROW_TILE
    n_in = w_in.shape[2]
    assert all(w == 2 ** (g + 1) for g, w in enumerate(POOL_WINDOWS)) and POOL_WINDOWS[-1] <= POOL_HALO
    row = lambda n: pl.BlockSpec((1, tm, n), lambda bi, i: (bi, i, 0))
    return pl.pallas_call(
        _inproj_kernel,
        grid=(b, s // tm),
        in_specs=[
            row(d),
            pl.BlockSpec((1, d, n_in), lambda *_: (layer, 0, 0), pipeline_mode=pl.Buffered(1)),
            _const_spec((1, 2 * D_MODEL)),
            pl.BlockSpec((tm, LANES), lambda bi, i: (i, 0)),
            pl.BlockSpec((tm, LANES), lambda bi, i: (i, 0)),
            _const_spec(w_pool.shape),
            _const_spec((1, D_POOL)),
        ],
        out_specs=[row(D_ATTN), row(D_ATTN), row(D_ATTN), row(D_POOL), row(2 * D_MODEL)],
        out_shape=[
            jax.ShapeDtypeStruct((b, s, D_ATTN), BF16),
            jax.ShapeDtypeStruct((b, s, D_ATTN), BF16),
            jax.ShapeDtypeStruct((b, s, D_ATTN), BF16),
            jax.ShapeDtypeStruct((b, s, D_POOL), BF16),
            jax.ShapeDtypeStruct((b, s, 2 * D_MODEL), BF16),
        ],
        scratch_shapes=[pltpu.VMEM((d, n_in), BF16)]
                       + [pltpu.VMEM((SUBLANES + POOL_HALO + tm, D_POOL - g * POOL_GROUP_DIM), F32)
                          for g in range(len(POOL_WINDOWS))],
        compiler_params=pltpu.CompilerParams(
            dimension_semantics=("arbitrary", "arbitrary"), vmem_limit_bytes=VMEM_LIMIT),
        name="inproj",
    )(x, w_in, b_gate, cos, sin, w_pool, pool_scale)


def _moba_kernel(n_cast, q_ref, k_ref, v_ref, *refs):
    w_refs, o_ref, wb_refs = refs[:n_cast], refs[n_cast], refs[n_cast + 1:2 * n_cast + 1]
    khat_ref, vt_ref, qt_ref, *s_refs = refs[2 * n_cast + 1:]
    for w_ref, wb_ref in zip(w_refs, wb_refs):
        wb_ref[...] = w_ref[0].astype(BF16)

    seq = k_ref.shape[1]
    nb = seq // MOBA_BLOCK
    blk = lambda j: slice(j * MOBA_BLOCK, (j + 1) * MOBA_BLOCK)
    lane = lax.broadcasted_iota(jnp.int32, (MOBA_BLOCK, LANES), 1)
    n_pairs = q_ref.shape[2] // LANES
    pair = lambda p: slice(p * LANES, (p + 1) * LANES)

    kmbs = []
    for p in range(n_pairs):
        kf = k_ref[0, :, pair(p)].astype(F32).reshape(nb, MOBA_BLOCK, LANES)
        km = jnp.sum(kf, axis=1) * (1.0 / MOBA_BLOCK)
        hi = km.astype(BF16)
        kmbs.append(jnp.concatenate([hi, (km - hi.astype(F32)).astype(BF16)], axis=0))

        for j in range(nb):
            kj = k_ref[0, blk(j), pair(p)]
            for h in range(HEADS_PER_VREG):
                g = p * HEADS_PER_VREG + h
                onehot = jnp.where(lane == (1 - h) * HEAD_DIM + j, 1.0, 0.0).astype(BF16)
                khat_ref[g, blk(j), :] = jnp.where((lane // HEAD_DIM) == h, kj, onehot)
            vt = v_ref[0, blk(j), pair(p)].astype(F32).T.astype(BF16)
            for h in range(HEADS_PER_VREG):
                g = p * HEADS_PER_VREG + h
                vt_ref[g, 0:HEAD_DIM, blk(j)] = vt[h * HEAD_DIM:(h + 1) * HEAD_DIM]
                vt_ref[g, HEAD_DIM:PV_ROWS, blk(j)] = jnp.ones((PV_ROWS - HEAD_DIM, MOBA_BLOCK), BF16)
            qt_ref[p, :, blk(j)] = q_ref[0, blk(j), pair(p)].astype(F32).T.astype(BF16)

    dim_i = lax.broadcasted_iota(jnp.int32, (LANES, MOBA_BLOCK), 0)
    key_i = lax.broadcasted_iota(jnp.int32, (MOBA_BLOCK, MOBA_BLOCK), 0)
    qry_i = lax.broadcasted_iota(jnp.int32, (MOBA_BLOCK, MOBA_BLOCK), 1)
    causal = key_i <= qry_i
    blk_i = lax.broadcasted_iota(jnp.int32, (nb, MOBA_BLOCK), 0)
    fold = lambda t: t.reshape(MOBA_BLOCK // SUBLANES, SUBLANES, MOBA_BLOCK)

    units = [(i, p * HEADS_PER_VREG + h)
             for p in range(n_pairs) for i in range(nb) for h in range(HEADS_PER_VREG)]
    state = {}
    sbuf = lambda u: s_refs[units.index(u) % len(s_refs)]

    def prepare(u):
        i, (p, h) = u[0], divmod(u[1], HEADS_PER_VREG)
        in_head = (dim_i // HEAD_DIM) == h
        q = qt_ref[p, :, blk(i)]
        if i > MOBA_TOPK:
            g2 = _dot(kmbs[p], jnp.where(in_head, q, jnp.zeros_like(q)))
            gate = g2[0:nb] + g2[nb:2 * nb]
            beaten = jnp.zeros((nb, MOBA_BLOCK), jnp.int32)
            for m in range(i):
                gm = gate[m:m + 1, :]
                wins = (gm > gate) | ((gm == gate) & (m < blk_i))
                beaten = beaten + jnp.where(wins, 1, 0)
            bias = jnp.where((blk_i < i) & (beaten >= MOBA_TOPK), NEG, 0.0)
            base = (1 - h) * HEAD_DIM
            parts = [bias, jnp.zeros((LANES - base - nb, MOBA_BLOCK), F32)]
            if base:
                parts = [jnp.zeros((base, MOBA_BLOCK), F32)] + parts
            spare = jnp.concatenate(parts, axis=0).astype(BF16)
        else:
            spare = jnp.zeros_like(q)
        state[u] = dict(q_aug=jnp.where(in_head, q, spare),
                        m_run=jnp.full((SUBLANES, MOBA_BLOCK), NEG, F32),
                        acc=jnp.zeros((PV_ROWS, MOBA_BLOCK), F32))

    def score_matmul(u):
        q_aug = state[u].pop("q_aug")
        n = u[0] + 1
        for lo in range(0, n, SCORE_SPLIT):
            keys = slice(lo * MOBA_BLOCK, min(lo + SCORE_SPLIT, n) * MOBA_BLOCK)
            sbuf(u)[keys, :] = _dot(khat_ref[u[1], keys, :], q_aug)

    def masked_scores(u, j):
        s = sbuf(u)[blk(j), :]
        return jnp.where(causal, s, NEG) if j == u[0] else s

    def score_step(u, j):
        st = state[u]
        st["m_run"] = jnp.maximum(st["m_run"], jnp.max(fold(masked_scores(u, j)), axis=0))

    def end_scores(u):
        state[u]["m_fin"] = jnp.max(state[u]["m_run"], axis=0, keepdims=True)

    def value_step(u, j):
        st = state[u]
        p = jnp.exp2(masked_scores(u, j) - st["m_fin"])
        st["acc"] = st["acc"] + _dot(vt_ref[u[1], :, blk(j)], p.astype(BF16))

    def finish(u):
        i, (p, h) = u[0], divmod(u[1], HEADS_PER_VREG)
        acc = state.pop(u)["acc"]
        out = acc[0:HEAD_DIM] / acc[HEAD_DIM:HEAD_DIM + 1]
        if h == 0:
            state["head0"] = out
        else:
            both = jnp.concatenate([state.pop("head0"), out], axis=0)
            o_ref[0, blk(i), pair(p)] = both.T.astype(BF16)

    def stage_tasks(t):
        unit = lambda d: units[t + d] if 0 <= t + d < len(units) else None
        stages = []
        if unit(3):
            stages.append([functools.partial(prepare, unit(3))])
        if unit(2):
            stages.append([functools.partial(score_matmul, unit(2))])
        if unit(1):
            stages.append([functools.partial(score_step, unit(1), j) for j in range(unit(1)[0] + 1)]
                          + [functools.partial(end_scores, unit(1))])
        if unit(0):
            stages.append([functools.partial(value_step, unit(0), j) for j in range(unit(0)[0] + 1)])
        return stages

    for t in range(-3, len(units)):
        stages = stage_tasks(t)
        n = max(len(tasks) for tasks in stages)
        for step in range(n):
            for tasks in stages:
                lo, hi_ = step * len(tasks) // n, (step + 1) * len(tasks) // n
                for task in tasks[lo:hi_]:
                    task()
        if t >= 0:
            finish(units[t])


def _cast_rows(rows, steps):
    r = -(-rows // steps)
    r = -(-r // BF16_ROWS) * BF16_ROWS
    while rows % r:
        r += BF16_ROWS
    return r


def _moba(q, k, v, layer, weights):
    b, s, _ = q.shape
    width = PAIRS_PER_STEP * LANES
    n_steps = D_ATTN // width
    full = pl.BlockSpec((1, s, width), lambda bi, p: (bi, 0, p))
    w_in_specs, w_out_specs, w_out_shapes = [], [], []
    for w in weights:
        _, rows, cols = w.shape
        r = _cast_rows(rows, b * n_steps)
        last = rows // r - 1
        w_in_specs.append(pl.BlockSpec(
            (1, r, cols), lambda bi, p, last=last: (layer, jnp.minimum(bi * n_steps + p, last), 0)))
        w_out_specs.append(pl.BlockSpec(
            (r, cols), lambda bi, p, last=last: (jnp.minimum(bi * n_steps + p, last), 0)))
        w_out_shapes.append(jax.ShapeDtypeStruct((rows, cols), BF16))
    y_attn, *w_bf16 = pl.pallas_call(
        functools.partial(_moba_kernel, len(weights)),
        grid=(b, n_steps),
        in_specs=[full, full, full] + w_in_specs,
        out_specs=[full] + w_out_specs,
        out_shape=[jax.ShapeDtypeStruct((b, s, D_ATTN), BF16)] + w_out_shapes,
        scratch_shapes=[pltpu.VMEM((PAIRS_PER_STEP * HEADS_PER_VREG, s, LANES), BF16),
                        pltpu.VMEM((PAIRS_PER_STEP * HEADS_PER_VREG, PV_ROWS, s), BF16),
                        pltpu.VMEM((PAIRS_PER_STEP, LANES, s), BF16),
                        *[pltpu.VMEM((s, MOBA_BLOCK), F32)] * SCORE_BUFFERS],
        compiler_params=pltpu.CompilerParams(
            dimension_semantics=("arbitrary", "arbitrary"), vmem_limit_bytes=VMEM_LIMIT),
        name="moba",
    )(q, k, v, *weights)
    return y_attn, w_bf16


def _mix_ffn_kernel(alpha, x_ref, ya_ref, pm_ref, g_ref, wa_ref, wp_ref, wo_ref, l1g_ref, l1b_ref,
                    wg_ref, wu_ref, cw_ref, cb_ref, wd_ref, l2g_ref, l2b_ref, o_ref,
                    x1buf, abuf, hbuf):
    i = pl.program_id(1)
    tm = x_ref.shape[1]
    halo = SUBLANES
    groups = [slice(r * LN_ROWS, (r + 1) * LN_ROWS) for r in range(tm // LN_ROWS)]

    @pl.when(i == 0)
    def _():
        abuf[0:halo, :] = jnp.zeros((halo, D_FF), F32)

    branches = [(_dot(ya_ref[0, rows, :], wa_ref[...]), _dot(pm_ref[0, rows, :], wp_ref[...]))
                for rows in groups]
    for rows, (y_attn, y_pool) in zip(groups, branches):
        g_attn = g_ref[0, rows, 0:D_MODEL].astype(F32)
        g_pool = g_ref[0, rows, D_MODEL:2 * D_MODEL].astype(F32)
        mix = _dot((g_attn * y_attn + g_pool * y_pool).astype(BF16), wo_ref[...])
        x1buf[rows, :] = _layer_norm(alpha * x_ref[0, rows, :] + mix, l1g_ref[...], l1b_ref[...])

    xb = x1buf[...].astype(BF16)
    for c in range(D_FF // FF_CHUNK):
        sl = slice(c * FF_CHUNK, (c + 1) * FF_CHUNK)
        if c == 0:
            parts = [(_dot(xg, wg_ref[:, sl]), _dot(xg, wu_ref[:, sl]))
                     for xg in (x1buf[rows, :].astype(BF16) for rows in groups)]
            a = jnp.concatenate([p[0] for p in parts])
            u = jnp.concatenate([p[1] for p in parts])
        else:
            a = _dot(xb, wg_ref[:, sl])
            u = _dot(xb, wu_ref[:, sl])
        abuf[halo:halo + tm, sl] = a
        conv = a * cw_ref[CONV_WIDTH - 1:CONV_WIDTH, sl] + cb_ref[:, sl]
        for t in range(CONV_WIDTH - 1):
            back = CONV_WIDTH - 1 - t
            conv = conv + abuf[halo - back:halo - back + tm, sl] * cw_ref[t:t + 1, sl]
        abuf[0:halo, sl] = a[tm - halo:tm, :]
        gelu = 0.5 * conv * (1.0 + lax.erf(conv * (2.0 ** -0.5)))
        hbuf[:, sl] = (gelu * u).astype(BF16)
    for rows in groups:
        ffn = _dot(hbuf[rows, :], wd_ref[...])
        o_ref[0, rows, :] = _layer_norm(alpha * x1buf[rows, :] + ffn, l2g_ref[...], l2b_ref[...])


def _mix_ffn(alpha, x, y_attn, pm, gates, w_a, w_p, w_o, ln1_g, ln1_b,
             w_g, w_u, conv_w, conv_b, w_d, ln2_g, ln2_b):
    b, s, d = x.shape
    tm = ROW_TILE
    row = lambda n: pl.BlockSpec((1, tm, n), lambda bi, i: (bi, i, 0))
    consts = [w_a, w_p, w_o, ln1_g, ln1_b, w_g, w_u, conv_w, conv_b, w_d, ln2_g, ln2_b]
    return pl.pallas_call(
        functools.partial(_mix_ffn_kernel, alpha),
        grid=(b, s // tm),
        in_specs=[row(d), row(D_ATTN), row(D_POOL), row(2 * D_MODEL)]
                 + [_const_spec(c.shape) for c in consts],
        out_specs=row(d),
        out_shape=jax.ShapeDtypeStruct((b, s, d), F32),
        scratch_shapes=[pltpu.VMEM((tm, d), F32),
                        pltpu.VMEM((SUBLANES + tm, D_FF), F32),
                        pltpu.VMEM((tm, D_FF), BF16)],
        compiler_params=pltpu.CompilerParams(
            dimension_semantics=("arbitrary", "arbitrary"), vmem_limit_bytes=VMEM_LIMIT),
        name="mix_ffn",
    )(x, y_attn, pm, gates, *consts)


def _rope_tables(s):
    half = HEAD_DIM // 2
    inv_freq = 1.0 / (ROPE_THETA ** (jnp.arange(half, dtype=F32) / half))
    ang = jnp.arange(s, dtype=F32)[:, None] * inv_freq[None, :]
    cos, sin = jnp.cos(ang), jnp.sin(ang)
    cos = jnp.tile(jnp.concatenate([cos, cos], axis=-1), (1, HEADS_PER_VREG))
    sin = jnp.tile(jnp.concatenate([-sin, sin], axis=-1), (1, HEADS_PER_VREG))
    return cos, sin


def kernel(x, w_in, b_gate, w_branch_attn, w_pool, pool_scale, w_branch_pool, w_out, ln1_g, ln1_b,
           w_ffn_gate, w_ffn_up, conv_w, conv_b, w_ffn_down, ln2_g, ln2_b):
    depth = w_in.shape[0]
    s = x.shape[1]
    alpha = (2.0 * depth) ** 0.25
    cos, sin = _rope_tables(s)
    vec = lambda t: t.reshape(1, -1)
    for l in range(depth):
        q, k, v, pm, gates = _inproj(x, l, w_in, vec(b_gate[l]), cos, sin,
                                     w_pool[l].astype(BF16), vec(pool_scale[l]))
        y_attn, (w_a, w_p, w_o, w_g, w_u, w_d) = _moba(
            q, k, v, l, [w_branch_attn, w_branch_pool, w_out, w_ffn_gate, w_ffn_up, w_ffn_down])
        x = _mix_ffn(alpha, x, y_attn, pm, gates, w_a, w_p, w_o, vec(ln1_g[l]), vec(ln1_b[l]),
                     w_g, w_u, conv_w[l], vec(conv_b[l]), w_d, vec(ln2_g[l]), vec(ln2_b[l]))
    return x
```

```python
import functools

import jax
import jax.numpy as jnp
from jax import lax
from jax.experimental import pallas as pl
from jax.experimental.pallas import tpu as pltpu

D_MODEL = 1024
ATTN_HEADS = 8
HEAD_DIM = 64
D_ATTN = ATTN_HEADS * HEAD_DIM
MOBA_BLOCK = 256
MOBA_TOPK = 3
ROPE_THETA = 10000.0
POOL_WINDOWS = (2, 4, 8, 16)
POOL_GROUP_DIM = 128
D_POOL = len(POOL_WINDOWS) * POOL_GROUP_DIM
D_FF = 2816
CONV_WIDTH = 3
LN_EPS = 1e-5
NEG = -1e30
Q_SCALE = HEAD_DIM ** -0.5 * 1.4426950408889634

LANES = 128
SUBLANES = 8
HEADS_PER_VREG = LANES // HEAD_DIM
BF16_ROWS = 2 * SUBLANES
PV_ROWS = HEAD_DIM + BF16_ROWS
SCORE_BUFFERS = 3
SCORE_SPLIT = 2
PAIRS_PER_STEP = 2
POOL_HALO = 16
ROW_TILE = 512
FF_CHUNK = 256
LN_ROWS = 256
VMEM_LIMIT = 52 * 1024 * 1024

F32 = jnp.float32
BF16 = jnp.bfloat16


def _dot(a, b):
    return jnp.dot(a, b, preferred_element_type=F32)


def _const_spec(shape):
    return pl.BlockSpec(shape, lambda *_: (0,) * len(shape), pipeline_mode=pl.Buffered(1))


def _layer_norm(y, g, b):
    mu = jnp.mean(y, axis=-1, keepdims=True)
    d = y - mu
    var = jnp.mean(d * d, axis=-1, keepdims=True)
    return d * lax.rsqrt(var + LN_EPS) * g + b


def _inproj_kernel(x_ref, wf_ref, bg_ref, cos_ref, sin_ref, wpool_ref, pscale_ref,
                   q_ref, k_ref, v_ref, pm_ref, g_ref, w_ref, ubuf, *lvl):
    i = pl.program_id(1)
    tm = x_ref.shape[1]

    @pl.when((pl.program_id(0) == 0) & (i == 0))
    def _():
        for c in range(0, w_ref.shape[1], D_ATTN):
            w_ref[:, c:c + D_ATTN] = wf_ref[0, :, c:c + D_ATTN].astype(BF16)

    xb = x_ref[0].astype(BF16)

    lane = lax.broadcasted_iota(jnp.int32, (tm, LANES), 1)
    first_half = (lane % HEAD_DIM) < (HEAD_DIM // 2)
    cos = cos_ref[...]
    sin = sin_ref[...]

    def rope(t):
        partner = jnp.where(first_half, pltpu.roll(t, LANES - HEAD_DIM // 2, 1),
                            pltpu.roll(t, HEAD_DIM // 2, 1))
        return t * cos + partner * sin

    pad, cur, end = SUBLANES, SUBLANES + POOL_HALO, SUBLANES + POOL_HALO + tm

    @pl.when(i == 0)
    def _():
        ubuf[0:cur, :] = jnp.zeros((cur, D_POOL), F32)
        for level in lvl:
            level[0:pad, :] = jnp.zeros((pad, level.shape[1]), F32)

    o3 = 3 * D_ATTN
    ubuf[cur:end, :] = _dot(xb, w_ref[:, o3:o3 + D_POOL])

    zq = _dot(xb, w_ref[:, 0:D_ATTN])
    for c in range(D_ATTN // LANES):
        sl = slice(c * LANES, (c + 1) * LANES)
        q_ref[0, :, sl] = (rope(zq[:, sl]) * Q_SCALE).astype(BF16)
    zk = _dot(xb, w_ref[:, D_ATTN:2 * D_ATTN])
    for c in range(D_ATTN // LANES):
        sl = slice(c * LANES, (c + 1) * LANES)
        k_ref[0, :, sl] = rope(zk[:, sl]).astype(BF16)
    v_ref[0] = _dot(xb, w_ref[:, 2 * D_ATTN:3 * D_ATTN]).astype(BF16)

    o4 = o3 + D_POOL
    gate_chunk = 2 * D_MODEL // len(POOL_WINDOWS)
    pos = i * tm + lax.broadcasted_iota(jnp.int32, (tm, 1), 0)
    prev = ubuf
    for g, w in enumerate(POOL_WINDOWS):
        sl = slice(g * gate_chunk, (g + 1) * gate_chunk)
        zg = _dot(xb, w_ref[:, o4 + g * gate_chunk:o4 + (g + 1) * gate_chunk]) + bg_ref[:, sl]
        g_ref[0, :, sl] = jax.nn.sigmoid(zg).astype(BF16)

        sl = slice(g * POOL_GROUP_DIM, (g + 1) * POOL_GROUP_DIM)
        sums = prev[pad:end, :] + prev[pad - w // 2:end - w // 2, :]
        if g + 1 < len(POOL_WINDOWS):
            lvl[g][pad:end, :] = sums[:, POOL_GROUP_DIM:]
            prev = lvl[g]
        inv_count = 1.0 / jnp.minimum(pos + 1, w).astype(F32)
        pooled = sums[POOL_HALO:, 0:POOL_GROUP_DIM] * inv_count - ubuf[cur:end, sl]
        mixed = _dot(pooled.astype(BF16), wpool_ref[g]) * pscale_ref[:, sl]
        pm_ref[0, :, sl] = mixed.astype(BF16)
    ubuf[pad:cur, :] = ubuf[end - POOL_HALO:end, :]


def _inproj(x, layer, w_in, b_gate, cos, sin, w_pool, pool_scale):
    b, s, d = x.shape
    tm = ROW_TILE
    n_in = w_in.shape[2]
    assert all(w == 2 ** (g + 1) for g, w in enumerate(POOL_WINDOWS)) and POOL_WINDOWS[-1] <= POOL_HALO
    row = lambda n: pl.BlockSpec((1, tm, n), lambda bi, i: (bi, i, 0))
    return pl.pallas_call(
        _inproj_kernel,
        grid=(b, s // tm),
        in_specs=[
            row(d),
            pl.BlockSpec((1, d, n_in), lambda *_: (layer, 0, 0), pipeline_mode=pl.Buffered(1)),
            _const_spec((1, 2 * D_MODEL)),
            pl.BlockSpec((tm, LANES), lambda bi, i: (i, 0)),
            pl.BlockSpec((tm, LANES), lambda bi, i: (i, 0)),
            _const_spec(w_pool.shape),
            _const_spec((1, D_POOL)),
        ],
        out_specs=[row(D_ATTN), row(D_ATTN), row(D_ATTN), row(D_POOL), row(2 * D_MODEL)],
        out_shape=[
            jax.ShapeDtypeStruct((b, s, D_ATTN), BF16),
            jax.ShapeDtypeStruct((b, s, D_ATTN), BF16),
            jax.ShapeDtypeStruct((b, s, D_ATTN), BF16),
            jax.ShapeDtypeStruct((b, s, D_POOL), BF16),
            jax.ShapeDtypeStruct((b, s, 2 * D_MODEL), BF16),
        ],
        scratch_shapes=[pltpu.VMEM((d, n_in), BF16)]
                       + [pltpu.VMEM((SUBLANES + POOL_HALO + tm, D_POOL - g * POOL_GROUP_DIM), F32)
                          for g in range(len(POOL_WINDOWS))],
        compiler_params=pltpu.CompilerParams(
            dimension_semantics=("arbitrary", "arbitrary"), vmem_limit_bytes=VMEM_LIMIT),
        name="inproj",
    )(x, w_in, b_gate, cos, sin, w_pool, pool_scale)


def _moba_kernel(n_cast, q_ref, k_ref, v_ref, *refs):
    w_refs, o_ref, wb_refs = refs[:n_cast], refs[n_cast], refs[n_cast + 1:2 * n_cast + 1]
    khat_ref, vt_ref, qt_ref, *s_refs = refs[2 * n_cast + 1:]
    for w_ref, wb_ref in zip(w_refs, wb_refs):
        wb_ref[...] = w_ref[0].astype(BF16)

    seq = k_ref.shape[1]
    nb = seq // MOBA_BLOCK
    blk = lambda j: slice(j * MOBA_BLOCK, (j + 1) * MOBA_BLOCK)
    lane = lax.broadcasted_iota(jnp.int32, (MOBA_BLOCK, LANES), 1)
    n_pairs = q_ref.shape[2] // LANES
    pair = lambda p: slice(p * LANES, (p + 1) * LANES)

    kmbs = []
    for p in range(n_pairs):
        kf = k_ref[0, :, pair(p)].astype(F32).reshape(nb, MOBA_BLOCK, LANES)
        km = jnp.sum(kf, axis=1) * (1.0 / MOBA_BLOCK)
        hi = km.astype(BF16)
        kmbs.append(jnp.concatenate([hi, (km - hi.astype(F32)).astype(BF16)], axis=0))

        for j in range(nb):
            kj = k_ref[0, blk(j), pair(p)]
            for h in range(HEADS_PER_VREG):
                g = p * HEADS_PER_VREG + h
                onehot = jnp.where(lane == (1 - h) * HEAD_DIM + j, 1.0, 0.0).astype(BF16)
                khat_ref[g, blk(j), :] = jnp.where((lane // HEAD_DIM) == h, kj, onehot)
            vt = v_ref[0, blk(j), pair(p)].astype(F32).T.astype(BF16)
            for h in range(HEADS_PER_VREG):
                g = p * HEADS_PER_VREG + h
                vt_ref[g, 0:HEAD_DIM, blk(j)] = vt[h * HEAD_DIM:(h + 1) * HEAD_DIM]
                vt_ref[g, HEAD_DIM:PV_ROWS, blk(j)] = jnp.ones((PV_ROWS - HEAD_DIM, MOBA_BLOCK), BF16)
            qt_ref[p, :, blk(j)] = q_ref[0, blk(j), pair(p)].astype(F32).T.astype(BF16)

    dim_i = lax.broadcasted_iota(jnp.int32, (LANES, MOBA_BLOCK), 0)
    key_i = lax.broadcasted_iota(jnp.int32, (MOBA_BLOCK, MOBA_BLOCK), 0)
    qry_i = lax.broadcasted_iota(jnp.int32, (MOBA_BLOCK, MOBA_BLOCK), 1)
    causal = key_i <= qry_i
    blk_i = lax.broadcasted_iota(jnp.int32, (nb, MOBA_BLOCK), 0)
    fold = lambda t: t.reshape(MOBA_BLOCK // SUBLANES, SUBLANES, MOBA_BLOCK)

    units = [(i, p * HEADS_PER_VREG + h)
             for p in range(n_pairs) for i in range(nb) for h in range(HEADS_PER_VREG)]
    state = {}
    sbuf = lambda u: s_refs[units.index(u) % len(s_refs)]

    def prepare(u):
        i, (p, h) = u[0], divmod(u[1], HEADS_PER_VREG)
        in_head = (dim_i // HEAD_DIM) == h
        q = qt_ref[p, :, blk(i)]
        if i > MOBA_TOPK:
            g2 = _dot(kmbs[p], jnp.where(in_head, q, jnp.zeros_like(q)))
            gate = g2[0:nb] + g2[nb:2 * nb]
            beaten = jnp.zeros((nb, MOBA_BLOCK), jnp.int32)
            for m in range(i):
                gm = gate[m:m + 1, :]
                wins = (gm > gate) | ((gm == gate) & (m < blk_i))
                beaten = beaten + jnp.where(wins, 1, 0)
            bias = jnp.where((blk_i < i) & (beaten >= MOBA_TOPK), NEG, 0.0)
            base = (1 - h) * HEAD_DIM
            parts = [bias, jnp.zeros((LANES - base - nb, MOBA_BLOCK), F32)]
            if base:
                parts = [jnp.zeros((base, MOBA_BLOCK), F32)] + parts
            spare = jnp.concatenate(parts, axis=0).astype(BF16)
        else:
            spare = jnp.zeros_like(q)
        state[u] = dict(q_aug=jnp.where(in_head, q, spare),
                        m_run=jnp.full((SUBLANES, MOBA_BLOCK), NEG, F32),
                        acc=jnp.zeros((PV_ROWS, MOBA_BLOCK), F32))

    def score_matmul(u):
        q_aug = state[u].pop("q_aug")
        n = u[0] + 1
        for lo in range(0, n, SCORE_SPLIT):
            keys = slice(lo * MOBA_BLOCK, min(lo + SCORE_SPLIT, n) * MOBA_BLOCK)
            sbuf(u)[keys, :] = _dot(khat_ref[u[1], keys, :], q_aug)

    def masked_scores(u, j):
        s = sbuf(u)[blk(j), :]
        return jnp.where(causal, s, NEG) if j == u[0] else s

    def score_step(u, j):
        st = state[u]
        st["m_run"] = jnp.maximum(st["m_run"], jnp.max(fold(masked_scores(u, j)), axis=0))

    def end_scores(u):
        state[u]["m_fin"] = jnp.max(state[u]["m_run"], axis=0, keepdims=True)

    def value_step(u, j):
        st = state[u]
        p = jnp.exp2(masked_scores(u, j) - st["m_fin"])
        st["acc"] = st["acc"] + _dot(vt_ref[u[1], :, blk(j)], p.astype(BF16))

    def finish(u):
        i, (p, h) = u[0], divmod(u[1], HEADS_PER_VREG)
        acc = state.pop(u)["acc"]
        out = acc[0:HEAD_DIM] / acc[HEAD_DIM:HEAD_DIM + 1]
        if h == 0:
            state["head0"] = out
        else:
            both = jnp.concatenate([state.pop("head0"), out], axis=0)
            o_ref[0, blk(i), pair(p)] = both.T.astype(BF16)

    def stage_tasks(t):
        unit = lambda d: units[t + d] if 0 <= t + d < len(units) else None
        stages = []
        if unit(3):
            stages.append([functools.partial(prepare, unit(3))])
        if unit(2):
            stages.append([functools.partial(score_matmul, unit(2))])
        if unit(1):
            stages.append([functools.partial(score_step, unit(1), j) for j in range(unit(1)[0] + 1)]
                          + [functools.partial(end_scores, unit(1))])
        if unit(0):
            stages.append([functools.partial(value_step, unit(0), j) for j in range(unit(0)[0] + 1)])
        return stages

    for t in range(-3, len(units)):
        stages = stage_tasks(t)
        n = max(len(tasks) for tasks in stages)
        for step in range(n):
            for tasks in stages:
                lo, hi_ = step * len(tasks) // n, (step + 1) * len(tasks) // n
                for task in tasks[lo:hi_]:
                    task()
        if t >= 0:
            finish(units[t])


def _cast_rows(rows, steps):
    r = -(-rows // steps)
    r = -(-r // BF16_ROWS) * BF16_ROWS
    while rows % r:
        r += BF16_ROWS
    return r


def _moba(q, k, v, layer, weights):
    b, s, _ = q.shape
    width = PAIRS_PER_STEP * LANES
    n_steps = D_ATTN // width
    full = pl.BlockSpec((1, s, width), lambda bi, p: (bi, 0, p))
    w_in_specs, w_out_specs, w_out_shapes = [], [], []
    for w in weights:
        _, rows, cols = w.shape
        r = _cast_rows(rows, b * n_steps)
        last = rows // r - 1
        w_in_specs.append(pl.BlockSpec(
            (1, r, cols), lambda bi, p, last=last: (layer, jnp.minimum(bi * n_steps + p, last), 0)))
        w_out_specs.append(pl.BlockSpec(
            (r, cols), lambda bi, p, last=last: (jnp.minimum(bi * n_steps + p, last), 0)))
        w_out_shapes.append(jax.ShapeDtypeStruct((rows, cols), BF16))
    y_attn, *w_bf16 = pl.pallas_call(
        functools.partial(_moba_kernel, len(weights)),
        grid=(b, n_steps),
        in_specs=[full, full, full] + w_in_specs,
        out_specs=[full] + w_out_specs,
        out_shape=[jax.ShapeDtypeStruct((b, s, D_ATTN), BF16)] + w_out_shapes,
        scratch_shapes=[pltpu.VMEM((PAIRS_PER_STEP * HEADS_PER_VREG, s, LANES), BF16),
                        pltpu.VMEM((PAIRS_PER_STEP * HEADS_PER_VREG, PV_ROWS, s), BF16),
                        pltpu.VMEM((PAIRS_PER_STEP, LANES, s), BF16),
                        *[pltpu.VMEM((s, MOBA_BLOCK), F32)] * SCORE_BUFFERS],
        compiler_params=pltpu.CompilerParams(
            dimension_semantics=("arbitrary", "arbitrary"), vmem_limit_bytes=VMEM_LIMIT),
        name="moba",
    )(q, k, v, *weights)
    return y_attn, w_bf16


def _mix_ffn_kernel(alpha, x_ref, ya_ref, pm_ref, g_ref, wa_ref, wp_ref, wo_ref, l1g_ref, l1b_ref,
                    wg_ref, wu_ref, cw_ref, cb_ref, wd_ref, l2g_ref, l2b_ref, o_ref,
                    x1buf, abuf, hbuf):
    i = pl.program_id(1)
    tm = x_ref.shape[1]
    halo = SUBLANES
    groups = [slice(r * LN_ROWS, (r + 1) * LN_ROWS) for r in range(tm // LN_ROWS)]

    @pl.when(i == 0)
    def _():
        abuf[0:halo, :] = jnp.zeros((halo, D_FF), F32)

    branches = [(_dot(ya_ref[0, rows, :], wa_ref[...]), _dot(pm_ref[0, rows, :], wp_ref[...]))
                for rows in groups]
    for rows, (y_attn, y_pool) in zip(groups, branches):
        g_attn = g_ref[0, rows, 0:D_MODEL].astype(F32)
        g_pool = g_ref[0, rows, D_MODEL:2 * D_MODEL].astype(F32)
        mix = _dot((g_attn * y_attn + g_pool * y_pool).astype(BF16), wo_ref[...])
        x1buf[rows, :] = _layer_norm(alpha * x_ref[0, rows, :] + mix, l1g_ref[...], l1b_ref[...])

    xb = x1buf[...].astype(BF16)
    for c in range(D_FF // FF_CHUNK):
        sl = slice(c * FF_CHUNK, (c + 1) * FF_CHUNK)
        if c == 0:
            parts = [(_dot(xg, wg_ref[:, sl]), _dot(xg, wu_ref[:, sl]))
                     for xg in (x1buf[rows, :].astype(BF16) for rows in groups)]
            a = jnp.concatenate([p[0] for p in parts])
            u = jnp.concatenate([p[1] for p in parts])
        else:
            a = _dot(xb, wg_ref[:, sl])
            u = _dot(xb, wu_ref[:, sl])
        abuf[halo:halo + tm, sl] = a
        conv = a * cw_ref[CONV_WIDTH - 1:CONV_WIDTH, sl] + cb_ref[:, sl]
        for t in range(CONV_WIDTH - 1):
            back = CONV_WIDTH - 1 - t
            conv = conv + abuf[halo - back:halo - back + tm, sl] * cw_ref[t:t + 1, sl]
        abuf[0:halo, sl] = a[tm - halo:tm, :]
        gelu = 0.5 * conv * (1.0 + lax.erf(conv * (2.0 ** -0.5)))
        hbuf[:, sl] = (gelu * u).astype(BF16)
    for rows in groups:
        ffn = _dot(hbuf[rows, :], wd_ref[...])
        o_ref[0, rows, :] = _layer_norm(alpha * x1buf[rows, :] + ffn, l2g_ref[...], l2b_ref[...])


def _mix_ffn(alpha, x, y_attn, pm, gates, w_a, w_p, w_o, ln1_g, ln1_b,
             w_g, w_u, conv_w, conv_b, w_d, ln2_g, ln2_b):
    b, s, d = x.shape
    tm = ROW_TILE
    row = lambda n: pl.BlockSpec((1, tm, n), lambda bi, i: (bi, i, 0))
    consts = [w_a, w_p, w_o, ln1_g, ln1_b, w_g, w_u, conv_w, conv_b, w_d, ln2_g, ln2_b]
    return pl.pallas_call(
        functools.partial(_mix_ffn_kernel, alpha),
        grid=(b, s // tm),
        in_specs=[row(d), row(D_ATTN), row(D_POOL), row(2 * D_MODEL)]
                 + [_const_spec(c.shape) for c in consts],
        out_specs=row(d),
        out_shape=jax.ShapeDtypeStruct((b, s, d), F32),
        scratch_shapes=[pltpu.VMEM((tm, d), F32),
                        pltpu.VMEM((SUBLANES + tm, D_FF), F32),
                        pltpu.VMEM((tm, D_FF), BF16)],
        compiler_params=pltpu.CompilerParams(
            dimension_semantics=("arbitrary", "arbitrary"), vmem_limit_bytes=VMEM_LIMIT),
        name="mix_ffn",
    )(x, y_attn, pm, gates, *consts)


def _rope_tables(s):
    half = HEAD_DIM // 2
    inv_freq = 1.0 / (ROPE_THETA ** (jnp.arange(half, dtype=F32) / half))
    ang = jnp.arange(s, dtype=F32)[:, None] * inv_freq[None, :]
    cos, sin = jnp.cos(ang), jnp.sin(ang)
    cos = jnp.tile(jnp.concatenate([cos, cos], axis=-1), (1, HEADS_PER_VREG))
    sin = jnp.tile(jnp.concatenate([-sin, sin], axis=-1), (1, HEADS_PER_VREG))
    return cos, sin


def kernel(x, w_in, b_gate, w_branch_attn, w_pool, pool_scale, w_branch_pool, w_out, ln1_g, ln1_b,
           w_ffn_gate, w_ffn_up, conv_w, conv_b, w_ffn_down, ln2_g, ln2_b):
    depth = w_in.shape[0]
    s = x.shape[1]
    alpha = (2.0 * depth) ** 0.25
    cos, sin = _rope_tables(s)
    vec = lambda t: t.reshape(1, -1)
    for l in range(depth):
        q, k, v, pm, gates = _inproj(x, l, w_in, vec(b_gate[l]), cos, sin,
                                     w_pool[l].astype(BF16), vec(pool_scale[l]))
        y_attn, (w_a, w_p, w_o, w_g, w_u, w_d) = _moba(
            q, k, v, l, [w_branch_attn, w_branch_pool, w_out, w_ffn_gate, w_ffn_up, w_ffn_down])
        x = _mix_ffn(alpha, x, y_attn, pm, gates, w_a, w_p, w_o, vec(ln1_g[l]), vec(ln1_b[l]),
                     w_g, w_u, conv_w[l], vec(conv_b[l]), w_d, vec(ln2_g[l]), vec(ln2_b[l]))
    return x
```

```python
import functools

import jax
import jax.numpy as jnp
from jax import lax
from jax.experimental import pallas as pl
from jax.experimental.pallas import tpu as pltpu

D_MODEL = 1024
ATTN_HEADS = 8
HEAD_DIM = 64
D_ATTN = ATTN_HEADS * HEAD_DIM
MOBA_BLOCK = 256
MOBA_TOPK = 3
ROPE_THETA = 10000.0
POOL_WINDOWS = (2, 4, 8, 16)
POOL_GROUP_DIM = 128
D_POOL = len(POOL_WINDOWS) * POOL_GROUP_DIM
D_FF = 2816
CONV_WIDTH = 3
LN_EPS = 1e-5
NEG = -1e30
Q_SCALE = HEAD_DIM ** -0.5 * 1.4426950408889634

LANES = 128
SUBLANES = 8
HEADS_PER_VREG = LANES // HEAD_DIM
BF16_ROWS = 2 * SUBLANES
PV_ROWS = HEAD_DIM + BF16_ROWS
SCORE_BUFFERS = 3
SCORE_SPLIT = 2
PAIRS_PER_STEP = 2
POOL_HALO = 16
ROW_TILE = 512
FF_CHUNK = 256
LN_ROWS = 256
VMEM_LIMIT = 52 * 1024 * 1024

F32 = jnp.float32
BF16 = jnp.bfloat16


def _dot(a, b):
    return jnp.dot(a, b, preferred_element_type=F32)


def _const_spec(shape):
    return pl.BlockSpec(shape, lambda *_: (0,) * len(shape), pipeline_mode=pl.Buffered(1))


def _layer_norm(y, g, b):
    mu = jnp.mean(y, axis=-1, keepdims=True)
    d = y - mu
    var = jnp.mean(d * d, axis=-1, keepdims=True)
    return d * lax.rsqrt(var + LN_EPS) * g + b


def _inproj_kernel(x_ref, wf_ref, bg_ref, cos_ref, sin_ref, wpool_ref, pscale_ref,
                   q_ref, k_ref, v_ref, pm_ref, g_ref, w_ref, ubuf, *lvl):
    i = pl.program_id(1)
    tm = x_ref.shape[1]

    @pl.when((pl.program_id(0) == 0) & (i == 0))
    def _():
        for c in range(0, w_ref.shape[1], D_ATTN):
            w_ref[:, c:c + D_ATTN] = wf_ref[0, :, c:c + D_ATTN].astype(BF16)

    xb = x_ref[0].astype(BF16)

    lane = lax.broadcasted_iota(jnp.int32, (tm, LANES), 1)
    first_half = (lane % HEAD_DIM) < (HEAD_DIM // 2)
    cos = cos_ref[...]
    sin = sin_ref[...]

    def rope(t):
        partner = jnp.where(first_half, pltpu.roll(t, LANES - HEAD_DIM // 2, 1),
                            pltpu.roll(t, HEAD_DIM // 2, 1))
        return t * cos + partner * sin

    pad, cur, end = SUBLANES, SUBLANES + POOL_HALO, SUBLANES + POOL_HALO + tm

    @pl.when(i == 0)
    def _():
        ubuf[0:cur, :] = jnp.zeros((cur, D_POOL), F32)
        for level in lvl:
            level[0:pad, :] = jnp.zeros((pad, level.shape[1]), F32)

    o3 = 3 * D_ATTN
    ubuf[cur:end, :] = _dot(xb, w_ref[:, o3:o3 + D_POOL])

    zq = _dot(xb, w_ref[:, 0:D_ATTN])
    for c in range(D_ATTN // LANES):
        sl = slice(c * LANES, (c + 1) * LANES)
        q_ref[0, :, sl] = (rope(zq[:, sl]) * Q_SCALE).astype(BF16)
    zk = _dot(xb, w_ref[:, D_ATTN:2 * D_ATTN])
    for c in range(D_ATTN // LANES):
        sl = slice(c * LANES, (c + 1) * LANES)
        k_ref[0, :, sl] = rope(zk[:, sl]).astype(BF16)
    v_ref[0] = _dot(xb, w_ref[:, 2 * D_ATTN:3 * D_ATTN]).astype(BF16)

    o4 = o3 + D_POOL
    gate_chunk = 2 * D_MODEL // len(POOL_WINDOWS)
    pos = i * tm + lax.broadcasted_iota(jnp.int32, (tm, 1), 0)
    prev = ubuf
    for g, w in enumerate(POOL_WINDOWS):
        sl = slice(g * gate_chunk, (g + 1) * gate_chunk)
        zg = _dot(xb, w_ref[:, o4 + g * gate_chunk:o4 + (g + 1) * gate_chunk]) + bg_ref[:, sl]
        g_ref[0, :, sl] = jax.nn.sigmoid(zg).astype(BF16)

        sl = slice(g * POOL_GROUP_DIM, (g + 1) * POOL_GROUP_DIM)
        sums = prev[pad:end, :] + prev[pad - w // 2:end - w // 2, :]
        if g + 1 < len(POOL_WINDOWS):
            lvl[g][pad:end, :] = sums[:, POOL_GROUP_DIM:]
            prev = lvl[g]
        inv_count = 1.0 / jnp.minimum(pos + 1, w).astype(F32)
        pooled = sums[POOL_HALO:, 0:POOL_GROUP_DIM] * inv_count - ubuf[cur:end, sl]
        mixed = _dot(pooled.astype(BF16), wpool_ref[g]) * pscale_ref[:, sl]
        pm_ref[0, :, sl] = mixed.astype(BF16)
    ubuf[pad:cur, :] = ubuf[end - POOL_HALO:end, :]


def _inproj(x, layer, w_in, b_gate, cos, sin, w_pool, pool_scale):
    b, s, d = x.shape
    tm = ROW_TILE
    n_in = w_in.shape[2]
    assert all(w == 2 ** (g + 1) for g, w in enumerate(POOL_WINDOWS)) and POOL_WINDOWS[-1] <= POOL_HALO
    row = lambda n: pl.BlockSpec((1, tm, n), lambda bi, i: (bi, i, 0))
    return pl.pallas_call(
        _inproj_kernel,
        grid=(b, s // tm),
        in_specs=[
            row(d),
            pl.BlockSpec((1, d, n_in), lambda *_: (layer, 0, 0), pipeline_mode=pl.Buffered(1)),
            _const_spec((1, 2 * D_MODEL)),
            pl.BlockSpec((tm, LANES), lambda bi, i: (i, 0)),
            pl.BlockSpec((tm, LANES), lambda bi, i: (i, 0)),
            _const_spec(w_pool.shape),
            _const_spec((1, D_POOL)),
        ],
        out_specs=[row(D_ATTN), row(D_ATTN), row(D_ATTN), row(D_POOL), row(2 * D_MODEL)],
        out_shape=[
            jax.ShapeDtypeStruct((b, s, D_ATTN), BF16),
            jax.ShapeDtypeStruct((b, s, D_ATTN), BF16),
            jax.ShapeDtypeStruct((b, s, D_ATTN), BF16),
            jax.ShapeDtypeStruct((b, s, D_POOL), BF16),
            jax.ShapeDtypeStruct((b, s, 2 * D_MODEL), BF16),
        ],
        scratch_shapes=[pltpu.VMEM((d, n_in), BF16)]
                       + [pltpu.VMEM((SUBLANES + POOL_HALO + tm, D_POOL - g * POOL_GROUP_DIM), F32)
                          for g in range(len(POOL_WINDOWS))],
        compiler_params=pltpu.CompilerParams(
            dimension_semantics=("arbitrary", "arbitrary"), vmem_limit_bytes=VMEM_LIMIT),
        name="inproj",
    )(x, w_in, b_gate, cos, sin, w_pool, pool_scale)


def _moba_kernel(n_cast, q_ref, k_ref, v_ref, *refs):
    w_refs, o_ref, wb_refs = refs[:n_cast], refs[n_cast], refs[n_cast + 1:2 * n_cast + 1]
    khat_ref, vt_ref, qt_ref, *s_refs = refs[2 * n_cast + 1:]
    for w_ref, wb_ref in zip(w_refs, wb_refs):
        wb_ref[...] = w_ref[0].astype(BF16)

    seq = k_ref.shape[1]
    nb = seq // MOBA_BLOCK
    blk = lambda j: slice(j * MOBA_BLOCK, (j + 1) * MOBA_BLOCK)
    lane = lax.broadcasted_iota(jnp.int32, (MOBA_BLOCK, LANES), 1)
    n_pairs = q_ref.shape[2] // LANES
    pair = lambda p: slice(p * LANES, (p + 1) * LANES)

    kmbs = []
    for p in range(n_pairs):
        kf = k_ref[0, :, pair(p)].astype(F32).reshape(nb, MOBA_BLOCK, LANES)
        km = jnp.sum(kf, axis=1) * (1.0 / MOBA_BLOCK)
        hi = km.astype(BF16)
        kmbs.append(jnp.concatenate([hi, (km - hi.astype(F32)).astype(BF16)], axis=0))

        for j in range(nb):
            kj = k_ref[0, blk(j), pair(p)]
            for h in range(HEADS_PER_VREG):
                g = p * HEADS_PER_VREG + h
                onehot = jnp.where(lane == (1 - h) * HEAD_DIM + j, 1.0, 0.0).astype(BF16)
                khat_ref[g, blk(j), :] = jnp.where((lane // HEAD_DIM) == h, kj, onehot)
            vt = v_ref[0, blk(j), pair(p)].astype(F32).T.astype(BF16)
            for h in range(HEADS_PER_VREG):
                g = p * HEADS_PER_VREG + h
                vt_ref[g, 0:HEAD_DIM, blk(j)] = vt[h * HEAD_DIM:(h + 1) * HEAD_DIM]
                vt_ref[g, HEAD_DIM:PV_ROWS, blk(j)] = jnp.ones((PV_ROWS - HEAD_DIM, MOBA_BLOCK), BF16)
            qt_ref[p, :, blk(j)] = q_ref[0, blk(j), pair(p)].astype(F32).T.astype(BF16)

    dim_i = lax.broadcasted_iota(jnp.int32, (LANES, MOBA_BLOCK), 0)
    key_i = lax.broadcasted_iota(jnp.int32, (MOBA_BLOCK, MOBA_BLOCK), 0)
    qry_i = lax.broadcasted_iota(jnp.int32, (MOBA_BLOCK, MOBA_BLOCK), 1)
    causal = key_i <= qry_i
    blk_i = lax.broadcasted_iota(jnp.int32, (nb, MOBA_BLOCK), 0)
    fold = lambda t: t.reshape(MOBA_BLOCK // SUBLANES, SUBLANES, MOBA_BLOCK)

    units = [(i, p * HEADS_PER_VREG + h)
             for p in range(n_pairs) for i in range(nb) for h in range(HEADS_PER_VREG)]
    state = {}
    sbuf = lambda u: s_refs[units.index(u) % len(s_refs)]

    def prepare(u):
        i, (p, h) = u[0], divmod(u[1], HEADS_PER_VREG)
        in_head = (dim_i // HEAD_DIM) == h
        q = qt_ref[p, :, blk(i)]
        if i > MOBA_TOPK:
            g2 = _dot(kmbs[p], jnp.where(in_head, q, jnp.zeros_like(q)))
            gate = g2[0:nb] + g2[nb:2 * nb]
            beaten = jnp.zeros((nb, MOBA_BLOCK), jnp.int32)
            for m in range(i):
                gm = gate[m:m + 1, :]
                wins = (gm > gate) | ((gm == gate) & (m < blk_i))
                beaten = beaten + jnp.where(wins, 1, 0)
            bias = jnp.where((blk_i < i) & (beaten >= MOBA_TOPK), NEG, 0.0)
            base = (1 - h) * HEAD_DIM
            parts = [bias, jnp.zeros((LANES - base - nb, MOBA_BLOCK), F32)]
            if base:
                parts = [jnp.zeros((base, MOBA_BLOCK), F32)] + parts
            spare = jnp.concatenate(parts, axis=0).astype(BF16)
        else:
            spare = jnp.zeros_like(q)
        state[u] = dict(q_aug=jnp.where(in_head, q, spare),
                        m_run=jnp.full((SUBLANES, MOBA_BLOCK), NEG, F32),
                        acc=jnp.zeros((PV_ROWS, MOBA_BLOCK), F32))

    def score_matmul(u):
        q_aug = state[u].pop("q_aug")
        n = u[0] + 1
        for lo in range(0, n, SCORE_SPLIT):
            keys = slice(lo * MOBA_BLOCK, min(lo + SCORE_SPLIT, n) * MOBA_BLOCK)
            sbuf(u)[keys, :] = _dot(khat_ref[u[1], keys, :], q_aug)

    def masked_scores(u, j):
        s = sbuf(u)[blk(j), :]
        return jnp.where(causal, s, NEG) if j == u[0] else s

    def score_step(u, j):
        st = state[u]
        st["m_run"] = jnp.maximum(st["m_run"], jnp.max(fold(masked_scores(u, j)), axis=0))

    def end_scores(u):
        state[u]["m_fin"] = jnp.max(state[u]["m_run"], axis=0, keepdims=True)

    def value_step(u, j):
        st = state[u]
        p = jnp.exp2(masked_scores(u, j) - st["m_fin"])
        st["acc"] = st["acc"] + _dot(vt_ref[u[1], :, blk(j)], p.astype(BF16))

    def finish(u):
        i, (p, h) = u[0], divmod(u[1], HEADS_PER_VREG)
        acc = state.pop(u)["acc"]
        out = acc[0:HEAD_DIM] / acc[HEAD_DIM:HEAD_DIM + 1]
        if h == 0:
            state["head0"] = out
        else:
            both = jnp.concatenate([state.pop("head0"), out], axis=0)
            o_ref[0, blk(i), pair(p)] = both.T.astype(BF16)

    def stage_tasks(t):
        unit = lambda d: units[t + d] if 0 <= t + d < len(units) else None
        stages = []
        if unit(3):
            stages.append([functools.partial(prepare, unit(3))])
        if unit(2):
            stages.append([functools.partial(score_matmul, unit(2))])
        if unit(1):
            stages.append([functools.partial(score_step, unit(1), j) for j in range(unit(1)[0] + 1)]
                          + [functools.partial(end_scores, unit(1))])
        if unit(0):
            stages.append([functools.partial(value_step, unit(0), j) for j in range(unit(0)[0] + 1)])
        return stages

    for t in range(-3, len(units)):
        stages = stage_tasks(t)
        n = max(len(tasks) for tasks in stages)
        for step in range(n):
            for tasks in stages:
                lo, hi_ = step * len(tasks) // n, (step + 1) * len(tasks) // n
                for task in tasks[lo:hi_]:
                    task()
        if t >= 0:
            finish(units[t])


def _cast_rows(rows, steps):
    r = -(-rows // steps)
    r = -(-r // BF16_ROWS) * BF16_ROWS
    while rows % r:
        r += BF16_ROWS
    return r


def _moba(q, k, v, layer, weights):
    b, s, _ = q.shape
    width = PAIRS_PER_STEP * LANES
    n_steps = D_ATTN // width
    full = pl.BlockSpec((1, s, width), lambda bi, p: (bi, 0, p))
    w_in_specs, w_out_specs, w_out_shapes = [], [], []
    for w in weights:
        _, rows, cols = w.shape
        r = _cast_rows(rows, b * n_steps)
        last = rows // r - 1
        w_in_specs.append(pl.BlockSpec(
            (1, r, cols), lambda bi, p, last=last: (layer, jnp.minimum(bi * n_steps + p, last), 0)))
        w_out_specs.append(pl.BlockSpec(
            (r, cols), lambda bi, p, last=last: (jnp.minimum(bi * n_steps + p, last), 0)))
        w_out_shapes.append(jax.ShapeDtypeStruct((rows, cols), BF16))
    y_attn, *w_bf16 = pl.pallas_call(
        functools.partial(_moba_kernel, len(weights)),
        grid=(b, n_steps),
        in_specs=[full, full, full] + w_in_specs,
        out_specs=[full] + w_out_specs,
        out_shape=[jax.ShapeDtypeStruct((b, s, D_ATTN), BF16)] + w_out_shapes,
        scratch_shapes=[pltpu.VMEM((PAIRS_PER_STEP * HEADS_PER_VREG, s, LANES), BF16),
                        pltpu.VMEM((PAIRS_PER_STEP * HEADS_PER_VREG, PV_ROWS, s), BF16),
                        pltpu.VMEM((PAIRS_PER_STEP, LANES, s), BF16),
                        *[pltpu.VMEM((s, MOBA_BLOCK), F32)] * SCORE_BUFFERS],
        compiler_params=pltpu.CompilerParams(
            dimension_semantics=("arbitrary", "arbitrary"), vmem_limit_bytes=VMEM_LIMIT),
        name="moba",
    )(q, k, v, *weights)
    return y_attn, w_bf16


def _mix_ffn_kernel(alpha, x_ref, ya_ref, pm_ref, g_ref, wa_ref, wp_ref, wo_ref, l1g_ref, l1b_ref,
                    wg_ref, wu_ref, cw_ref, cb_ref, wd_ref, l2g_ref, l2b_ref, o_ref,
                    x1buf, abuf, hbuf):
    i = pl.program_id(1)
    tm = x_ref.shape[1]
    halo = SUBLANES
    groups = [slice(r * LN_ROWS, (r + 1) * LN_ROWS) for r in range(tm // LN_ROWS)]

    @pl.when(i == 0)
    def _():
        abuf[0:halo, :] = jnp.zeros((halo, D_FF), F32)

    branches = [(_dot(ya_ref[0, rows, :], wa_ref[...]), _dot(pm_ref[0, rows, :], wp_ref[...]))
                for rows in groups]
    for rows, (y_attn, y_pool) in zip(groups, branches):
        g_attn = g_ref[0, rows, 0:D_MODEL].astype(F32)
        g_pool = g_ref[0, rows, D_MODEL:2 * D_MODEL].astype(F32)
        mix = _dot((g_attn * y_attn + g_pool * y_pool).astype(BF16), wo_ref[...])
        x1buf[rows, :] = _layer_norm(alpha * x_ref[0, rows, :] + mix, l1g_ref[...], l1b_ref[...])

    xb = x1buf[...].astype(BF16)
    for c in range(D_FF // FF_CHUNK):
        sl = slice(c * FF_CHUNK, (c + 1) * FF_CHUNK)
        if c == 0:
            parts = [(_dot(xg, wg_ref[:, sl]), _dot(xg, wu_ref[:, sl]))
                     for xg in (x1buf[rows, :].astype(BF16) for rows in groups)]
            a = jnp.concatenate([p[0] for p in parts])
            u = jnp.concatenate([p[1] for p in parts])
        else:
            a = _dot(xb, wg_ref[:, sl])
            u = _dot(xb, wu_ref[:, sl])
        abuf[halo:halo + tm, sl] = a
        conv = a * cw_ref[CONV_WIDTH - 1:CONV_WIDTH, sl] + cb_ref[:, sl]
        for t in range(CONV_WIDTH - 1):
            back = CONV_WIDTH - 1 - t
            conv = conv + abuf[halo - back:halo - back + tm, sl] * cw_ref[t:t + 1, sl]
        abuf[0:halo, sl] = a[tm - halo:tm, :]
        gelu = 0.5 * conv * (1.0 + lax.erf(conv * (2.0 ** -0.5)))
        hbuf[:, sl] = (gelu * u).astype(BF16)
    half = LN_ROWS // 2
    tail = [slice(tm - 2 * half, tm - half), slice(tm - half, tm)]
    for rows in groups[:-1] + tail:
        ffn = _dot(hbuf[rows, :], wd_ref[...])
        o_ref[0, rows, :] = _layer_norm(alpha * x1buf[rows, :] + ffn, l2g_ref[...], l2b_ref[...])


def _mix_ffn(alpha, x, y_attn, pm, gates, w_a, w_p, w_o, ln1_g, ln1_b,
             w_g, w_u, conv_w, conv_b, w_d, ln2_g, ln2_b):
    b, s, d = x.shape
    tm = ROW_TILE
    row = lambda n: pl.BlockSpec((1, tm, n), lambda bi, i: (bi, i, 0))
    consts = [w_a, w_p, w_o, ln1_g, ln1_b, w_g, w_u, conv_w, conv_b, w_d, ln2_g, ln2_b]
    return pl.pallas_call(
        functools.partial(_mix_ffn_kernel, alpha),
        grid=(b, s // tm),
        in_specs=[row(d), row(D_ATTN), row(D_POOL), row(2 * D_MODEL)]
                 + [_const_spec(c.shape) for c in consts],
        out_specs=row(d),
        out_shape=jax.ShapeDtypeStruct((b, s, d), F32),
        scratch_shapes=[pltpu.VMEM((tm, d), F32),
                        pltpu.VMEM((SUBLANES + tm, D_FF), F32),
                        pltpu.VMEM((tm, D_FF), BF16)],
        compiler_params=pltpu.CompilerParams(
            dimension_semantics=("arbitrary", "arbitrary"), vmem_limit_bytes=VMEM_LIMIT),
        name="mix_ffn",
    )(x, y_attn, pm, gates, *consts)


def _rope_tables(s):
    half = HEAD_DIM // 2
    inv_freq = 1.0 / (ROPE_THETA ** (jnp.arange(half, dtype=F32) / half))
    ang = jnp.arange(s, dtype=F32)[:, None] * inv_freq[None, :]
    cos, sin = jnp.cos(ang), jnp.sin(ang)
    cos = jnp.tile(jnp.concatenate([cos, cos], axis=-1), (1, HEADS_PER_VREG))
    sin = jnp.tile(jnp.concatenate([-sin, sin], axis=-1), (1, HEADS_PER_VREG))
    return cos, sin


def kernel(x, w_in, b_gate, w_branch_attn, w_pool, pool_scale, w_branch_pool, w_out, ln1_g, ln1_b,
           w_ffn_gate, w_ffn_up, conv_w, conv_b, w_ffn_down, ln2_g, ln2_b):
    depth = w_in.shape[0]
    s = x.shape[1]
    alpha = (2.0 * depth) ** 0.25
    cos, sin = _rope_tables(s)
    vec = lambda t: t.reshape(1, -1)
    for l in range(depth):
        q, k, v, pm, gates = _inproj(x, l, w_in, vec(b_gate[l]), cos, sin,
                                     w_pool[l].astype(BF16), vec(pool_scale[l]))
        y_attn, (w_a, w_p, w_o, w_g, w_u, w_d) = _moba(
            q, k, v, l, [w_branch_attn, w_branch_pool, w_out, w_ffn_gate, w_ffn_up, w_ffn_down])
        x = _mix_ffn(alpha, x, y_attn, pm, gates, w_a, w_p, w_o, vec(ln1_g[l]), vec(ln1_b[l]),
                     w_g, w_u, conv_w[l], vec(conv_b[l]), w_d, vec(ln2_g[l]), vec(ln2_b[l]))
    return x
```
